```python
import math
import jax, jax.numpy as jnp
from jax import lax
import numpy as np

D_MODEL = 1024
BATCH = 4
SEQ = 8192
DEPTH = 4

PLE_DIM = 256
CONV_DIM = 512
CONV_WIDTH = 31
RET_HEADS = 8
RET_DK = 64
RET_DV = 128
RET_CHUNK = 128
ROPE_BASE = 10000.0
D_FF_DENSE = 2816
N_EXPERTS = 8
TOP_K = 2
D_FF_EXPERT = 3584
MOE_BLOCK = 256
EPS = 1e-6
N_DENSE = (DEPTH + 1) // 2
N_MOE = DEPTH // 2

SPLIT_SIZES = (2 * CONV_DIM,
               RET_HEADS * RET_DK,
               RET_HEADS * RET_DK,
               RET_HEADS * RET_DV,
               RET_HEADS * RET_DV,
               D_MODEL,
               D_MODEL)
IN_COLS = sum(SPLIT_SIZES)

kernel_name = "hybrid_conv_retention_moe_ple"


def rmsnorm(x, g):
    x32 = x.astype(jnp.float32)
    y = x32 * lax.rsqrt(jnp.mean(x32 * x32, axis=-1, keepdims=True) + EPS)
    return (y * g.astype(jnp.float32)).astype(x.dtype)


def layernorm(x, g, b):
    x32 = x.astype(jnp.float32)
    mu = jnp.mean(x32, axis=-1, keepdims=True)
    var = jnp.mean(jnp.square(x32 - mu), axis=-1, keepdims=True)
    y = (x32 - mu) * lax.rsqrt(var + EPS)
    return (y * g.astype(jnp.float32) + b.astype(jnp.float32)).astype(x.dtype)


def head_norm(y):
    y32 = y.astype(jnp.float32)
    mu = jnp.mean(y32, axis=-1, keepdims=True)
    var = jnp.mean(jnp.square(y32 - mu), axis=-1, keepdims=True)
    return ((y32 - mu) * lax.rsqrt(var + EPS)).astype(y.dtype)


def rotary(t, cos, sin):
    half = t.shape[-1] // 2
    t1, t2 = t[..., :half], t[..., half:]
    return jnp.concatenate([t1 * cos - t2 * sin, t1 * sin + t2 * cos], axis=-1)


def conv_branch(u, conv_w, conv_b, ln_g, ln_b):
    a, gate = jnp.split(u, 2, axis=-1)
    z = a * jax.nn.sigmoid(gate)
    z = lax.conv_general_dilated(
        z, conv_w[:, None, :].astype(z.dtype), window_strides=(1,),
        padding=[(CONV_WIDTH - 1, 0)],
        dimension_numbers=("NWC", "WIO", "NWC"),
        feature_group_count=CONV_DIM) + conv_b
    z = layernorm(z, ln_g, ln_b)
    return jax.nn.silu(z)


def retention(q, k, v):
    B, S, H, DK = q.shape
    DV = v.shape[-1]
    C = RET_CHUNK
    N = S // C
    dt = q.dtype
    log_g = jnp.log(1.0 - jnp.exp2(-5.0 - jnp.arange(H, dtype=jnp.float32)))
    pos = jnp.arange(C, dtype=jnp.float32)
    diff = pos[:, None] - pos[None, :]
    dmat = jnp.where(diff[None] >= 0,
                     jnp.exp(jnp.maximum(diff, 0.0)[None] * log_g[:, None, None]),
                     0.0).astype(dt)
    xi = jnp.exp((pos + 1.0)[None] * log_g[:, None]).astype(dt)
    zeta = jnp.exp((C - 1.0 - pos)[None] * log_g[:, None]).astype(dt)
    g_chunk = jnp.exp(C * log_g).astype(dt)

    def chunks(t):
        return t.reshape(B, N, C, H, t.shape[-1]).transpose(1, 0, 3, 2, 4)

    qc, kc, vc = chunks(q), chunks(k), chunks(v)
    s = jnp.einsum('nbhik,nbhjk->nbhij', qc, kc) * dmat
    intra = jnp.einsum('nbhij,nbhjv->nbhiv', s, vc)
    kz = kc * zeta[None, None, :, :, None]

    def step(state, inp):
        qn, kn, vn = inp
        cross = jnp.einsum('bhik,bhkv->bhiv', qn, state) * xi[None, :, :, None]
        state = g_chunk[None, :, None, None] * state + jnp.einsum('bhjk,bhjv->bhkv', kn, vn)
        return state, cross

    state0 = jnp.zeros((B, H, DK, DV), dt)
    _, cross = lax.scan(step, state0, (qc, kz, vc))
    o = intra + cross
    return o.transpose(1, 0, 3, 2, 4).reshape(B, S, H, DV)


def swiglu(h, w_gate, w_up, w_down):
    return (jax.nn.silu(h @ w_gate) * (h @ w_up)) @ w_down


def moe_ffn(h, w_router, w_gate, w_up, w_down):
    B, S, D = h.shape
    T = B * S
    hf = h.reshape(T, D)
    logits = hf.astype(jnp.float32) @ w_router.astype(jnp.float32)
    top_val, top_idx = lax.top_k(logits, TOP_K)
    gates = jax.nn.softmax(top_val, axis=-1).astype(h.dtype)
    TK = T * TOP_K
    flat_e = top_idx.reshape(TK)
    flat_tok = jnp.repeat(jnp.arange(T, dtype=jnp.int32), TOP_K)
    flat_g = gates.reshape(TK)
    order = jnp.argsort(flat_e)
    e_sorted = flat_e[order]
    counts = jnp.bincount(flat_e, length=N_EXPERTS)
    start = jnp.cumsum(counts) - counts
    padded = ((counts + MOE_BLOCK - 1) // MOE_BLOCK) * MOE_BLOCK
    pad_end = jnp.cumsum(padded)
    pad_start = pad_end - padded
    dest = pad_start[e_sorted] + (jnp.arange(TK) - start[e_sorted])
    n_rows = ((TK + N_EXPERTS * (MOE_BLOCK - 1) + MOE_BLOCK - 1) // MOE_BLOCK) * MOE_BLOCK
    n_blocks = n_rows // MOE_BLOCK
    row_tok = jnp.zeros((n_rows,), jnp.int32).at[dest].set(flat_tok[order])
    row_gate = jnp.zeros((n_rows,), h.dtype).at[dest].set(flat_g[order])
    block_start = jnp.arange(n_blocks) * MOE_BLOCK
    block_e = jnp.minimum(jnp.searchsorted(pad_end, block_start, side='right'), N_EXPERTS - 1)
    xin = hf[row_tok].reshape(n_blocks, MOE_BLOCK, D)

    def run_block(args):
        xb, e = args
        return swiglu(xb, w_gate[e], w_up[e], w_down[e])

    y = lax.map(run_block, (xin, block_e)).reshape(n_rows, D)
    out = jnp.zeros((T, D), h.dtype).at[row_tok].add(y * row_gate[:, None])
    return out.reshape(B, S, D)


def setup_inputs(seed: int = 0) -> dict:
    key = jax.random.key(seed)
    ks = iter(jax.random.split(key, 32))
    f32 = jnp.float32

    def w(shape, fan_in):
        return jax.random.normal(next(ks), shape, f32) * (fan_in ** -0.5)

    def gain(shape):
        return 1.0 + 0.02 * jax.random.normal(next(ks), shape, f32)

    def bias(shape):
        return 0.02 * jax.random.normal(next(ks), shape, f32)

    return {
        "x": jax.random.normal(next(ks), (BATCH, SEQ, D_MODEL), f32),
        "p": jax.random.normal(next(ks), (DEPTH, BATCH, SEQ, PLE_DIM), f32),
        "g_mix": gain((DEPTH, D_MODEL)),
        "w_in": w((DEPTH, D_MODEL, IN_COLS), D_MODEL),
        "conv_w": w((DEPTH, CONV_WIDTH, CONV_DIM), CONV_WIDTH),
        "conv_b": bias((DEPTH, CONV_DIM)),
        "conv_ln_g": gain((DEPTH, CONV_DIM)),
        "conv_ln_b": bias((DEPTH, CONV_DIM)),
        "w_conv_out": w((DEPTH, CONV_DIM, D_MODEL), CONV_DIM),
        "w_ret_out": w((DEPTH, RET_HEADS * RET_DV, D_MODEL), RET_HEADS * RET_DV),
        "w_out": w((DEPTH, D_MODEL, D_MODEL), D_MODEL),
        "g_ffn": gain((DEPTH, D_MODEL)),
        "w_dense_gate": w((N_DENSE, D_MODEL, D_FF_DENSE), D_MODEL),
        "w_dense_up": w((N_DENSE, D_MODEL, D_FF_DENSE), D_MODEL),
        "w_dense_down": w((N_DENSE, D_FF_DENSE, D_MODEL), D_FF_DENSE),
        "w_router": w((N_MOE, D_MODEL, N_EXPERTS), D_MODEL),
        "w_exp_gate": w((N_MOE, N_EXPERTS, D_MODEL, D_FF_EXPERT), D_MODEL),
        "w_exp_up": w((N_MOE, N_EXPERTS, D_MODEL, D_FF_EXPERT), D_MODEL),
        "w_exp_down": w((N_MOE, N_EXPERTS, D_FF_EXPERT, D_MODEL), D_FF_EXPERT),
        "g_ple": gain((DEPTH, D_MODEL)),
        "w_ple_gate": w((DEPTH, D_MODEL, D_MODEL), D_MODEL),
        "w_ple_proj": w((DEPTH, PLE_DIM, D_MODEL), PLE_DIM),
        "g_final": gain((D_MODEL,)),
    }


def reference(x, p, g_mix, w_in, conv_w, conv_b, conv_ln_g, conv_ln_b, w_conv_out,
              w_ret_out, w_out, g_ffn, w_dense_gate, w_dense_up, w_dense_down,
              w_router, w_exp_gate, w_exp_up, w_exp_down, g_ple, w_ple_gate,
              w_ple_proj, g_final):
    B, S, D = x.shape
    dt = x.dtype
    inv_freq = ROPE_BASE ** (-jnp.arange(0, RET_DK, 2, dtype=jnp.float32) / RET_DK)
    ang = jnp.arange(S, dtype=jnp.float32)[:, None] * inv_freq[None, :]
    cos = jnp.cos(ang)[:, None, :].astype(dt)
    sin = jnp.sin(ang)[:, None, :].astype(dt)
    split_idx = list(np.cumsum(SPLIT_SIZES)[:-1])

    for i in range(DEPTH):
        h = rmsnorm(x, g_mix[i])
        u_conv, q, k, v, g_ret, ga, gb = jnp.split(h @ w_in[i], split_idx, axis=-1)
        y_conv = conv_branch(u_conv, conv_w[i], conv_b[i], conv_ln_g[i], conv_ln_b[i]) @ w_conv_out[i]
        q = rotary(q.reshape(B, S, RET_HEADS, RET_DK), cos, sin) * (RET_DK ** -0.5)
        k = rotary(k.reshape(B, S, RET_HEADS, RET_DK), cos, sin)
        v = v.reshape(B, S, RET_HEADS, RET_DV)
        o = head_norm(retention(q, k, v)).reshape(B, S, RET_HEADS * RET_DV)
        y_ret = (jax.nn.silu(g_ret) * o) @ w_ret_out[i]
        merged = jax.nn.sigmoid(ga) * y_conv + jax.nn.sigmoid(gb) * y_ret
        x = x + merged @ w_out[i]

        h2 = rmsnorm(x, g_ffn[i])
        if i % 2 == 0:
            j = i // 2
            x = x + swiglu(h2, w_dense_gate[j], w_dense_up[j], w_dense_down[j])
        else:
            j = i // 2
            x = x + moe_ffn(h2, w_router[j], w_exp_gate[j], w_exp_up[j], w_exp_down[j])

        gate = jax.nn.sigmoid(rmsnorm(x, g_ple[i]) @ w_ple_gate[i])
        x = x + gate * (p[i] @ w_ple_proj[i])

    return rmsnorm(x, g_final)
```

```python
import functools

import numpy as np
import jax
import jax.numpy as jnp
from jax import lax
from jax.experimental import pallas as pl
from jax.experimental.pallas import tpu as pltpu

F32 = jnp.float32
BF16 = jnp.bfloat16

EPS = 1e-6
CONV_WIDTH = 31
RET_HEADS = 8
RET_DK = 64
RET_DV = 128
RET_CHUNK = 128
ROPE_BASE = 10000.0
N_EXPERTS = 8

V7X_VMEM_LIMIT_BYTES = 56 * 1024 * 1024
LANES = 128
CONV_HALO = 32
TOK_TILE = 256
ROW_WIN = 256
ROW_BLK = 512
COMBINE_VISITS = 2 * N_EXPERTS


def _cparams(*sem):
    return pltpu.CompilerParams(dimension_semantics=sem, vmem_limit_bytes=V7X_VMEM_LIMIT_BYTES)


def _rms(x, g):
    return x * lax.rsqrt(jnp.mean(x * x, axis=-1, keepdims=True) + EPS) * g


def _silu(x):
    return x * jax.nn.sigmoid(x)


def _inproj_kernel(x_ref, g_ref, w_ref, o_ref, h_ref):
    @pl.when(pl.program_id(1) == 0)
    def _():
        h_ref[...] = _rms(x_ref[...], g_ref[...]).astype(BF16)

    o_ref[...] = jnp.dot(h_ref[...], w_ref[...], preferred_element_type=F32).astype(BF16)


def _inproj(x, g, w, tm=1024, tn=1024):
    T, D = x.shape
    N = w.shape[1]
    tm = min(tm, T)
    return pl.pallas_call(
        _inproj_kernel,
        grid=(T // tm, N // tn),
        in_specs=[pl.BlockSpec((tm, D), lambda i, j: (i, 0)),
                  pl.BlockSpec((1, D), lambda i, j: (0, 0)),
                  pl.BlockSpec((D, tn), lambda i, j: (0, j))],
        out_specs=pl.BlockSpec((tm, tn), lambda i, j: (i, j)),
        out_shape=jax.ShapeDtypeStruct((T, N), BF16),
        scratch_shapes=[pltpu.VMEM((tm, D), BF16)],
        compiler_params=_cparams("parallel", "arbitrary"),
        name="inproj",
    )(x, g.reshape(1, D), w)


def _conv_kernel(cur_ref, halo_ref, w_ref, b_ref, lg_ref, lb_ref, o_ref, zbuf, cbuf, *, ts, cc, rc):
    cur = cur_ref[0].astype(F32)
    zbuf[CONV_HALO:, :] = cur[:, :cc] * jax.nn.sigmoid(cur[:, cc:])
    halo = halo_ref[0].astype(F32)
    zh = halo[:, :cc] * jax.nn.sigmoid(halo[:, cc:])
    zbuf[:CONV_HALO, :] = jnp.where(pl.program_id(1) == 0, 0.0, zh)
    first = CONV_HALO - (CONV_WIDTH - 1)
    for c in range(cc // LANES):
        cs = slice(c * LANES, (c + 1) * LANES)
        for r in range(ts // rc):
            acc = jnp.broadcast_to(b_ref[:, cs], (rc, LANES))
            for j in range(CONV_WIDTH):
                lo = r * rc + first + j
                acc = acc + w_ref[j:j + 1, cs] * zbuf[lo:lo + rc, cs]
            cbuf[r * rc:(r + 1) * rc, cs] = acc
    z = cbuf[...]
    mu = jnp.mean(z, axis=-1, keepdims=True)
    var = jnp.mean(jnp.square(z - mu), axis=-1, keepdims=True)
    y = (z - mu) * lax.rsqrt(var + EPS) * lg_ref[...] + lb_ref[...]
    o_ref[0] = _silu(y).astype(BF16)


def _conv_branch(u3, conv_w, conv_b, ln_g, ln_b, ts=256, rc=32):
    B, S, _ = u3.shape
    cc = conv_w.shape[1]
    ts = min(ts, S)
    hb = ts // CONV_HALO
    return pl.pallas_call(
        functools.partial(_conv_kernel, ts=ts, cc=cc, rc=rc),
        grid=(B, S // ts),
        in_specs=[pl.BlockSpec((1, ts, 2 * cc), lambda b, i: (b, i, 0)),
                  pl.BlockSpec((1, CONV_HALO, 2 * cc), lambda b, i: (b, jnp.maximum(i * hb - 1, 0), 0)),
                  pl.BlockSpec((CONV_WIDTH, cc), lambda b, i: (0, 0)),
                  pl.BlockSpec((1, cc), lambda b, i: (0, 0)),
                  pl.BlockSpec((1, cc), lambda b, i: (0, 0)),
                  pl.BlockSpec((1, cc), lambda b, i: (0, 0))],
        out_specs=pl.BlockSpec((1, ts, cc), lambda b, i: (b, i, 0)),
        out_shape=jax.ShapeDtypeStruct((B, S, cc), BF16),
        scratch_shapes=[pltpu.VMEM((ts + CONV_HALO, cc), F32), pltpu.VMEM((ts, cc), F32)],
        compiler_params=_cparams("parallel", "parallel"),
        name="conv_branch",
    )(u3, u3, conv_w, conv_b.reshape(1, cc), ln_g.reshape(1, cc), ln_b.reshape(1, cc))


def _retention_kernel(q_ref, k_ref, v_ref, gr_ref, cos_ref, sa_ref, sb_ref, dmat_ref, xi_ref, zeta_ref,
                      o_ref, state, *, g_chunk):
    @pl.when(pl.program_id(1) == 0)
    def _():
        state[...] = jnp.zeros_like(state)

    hdk = RET_HEADS * RET_DK
    cos, sa, sb = cos_ref[...], sa_ref[...], sb_ref[...]

    def rope(t):
        return t * cos + pltpu.roll(t, hdk - RET_DK // 2, 1) * sa + pltpu.roll(t, RET_DK // 2, 1) * sb

    q = (rope(q_ref[0].astype(F32)) * (RET_DK ** -0.5)).astype(BF16)
    k = rope(k_ref[0].astype(F32))
    kz = (k * zeta_ref[...]).astype(BF16)
    k = k.astype(BF16)
    v = v_ref[0]
    for h in range(RET_HEADS):
        ks = slice(h * RET_DK, (h + 1) * RET_DK)
        vs = slice(h * RET_DV, (h + 1) * RET_DV)
        qh, kh, vh = q[:, ks], k[:, ks], v[:, vs]
        s = lax.dot_general(qh, kh, (((1,), (1,)), ((), ())), preferred_element_type=F32) * dmat_ref[h]
        intra = jnp.dot(s.astype(BF16), vh, preferred_element_type=F32)
        st = state[h]
        cross = jnp.dot(qh, st.astype(BF16), preferred_element_type=F32) * xi_ref[:, vs]
        state[h] = g_chunk[h] * st + lax.dot_general(kz[:, ks], vh, (((0,), (0,)), ((), ())),
                                                     preferred_element_type=F32)
        o = intra + cross
        mu = jnp.mean(o, axis=-1, keepdims=True)
        var = jnp.mean(jnp.square(o - mu), axis=-1, keepdims=True)
        o = (o - mu) * lax.rsqrt(var + EPS)
        o_ref[0, :, vs] = (_silu(gr_ref[0, :, vs].astype(F32)) * o).astype(BF16)


def _retention_tables(S):
    C, H = RET_CHUNK, RET_HEADS
    inv_freq = ROPE_BASE ** (-jnp.arange(0, RET_DK, 2, dtype=F32) / RET_DK)
    ang = jnp.arange(S, dtype=F32)[:, None] * inv_freq[None, :]
    cos, sin = jnp.cos(ang), jnp.sin(ang)
    zero = jnp.zeros_like(sin)
    cos_f = jnp.tile(cos, (1, 2 * H))
    sin_a = jnp.tile(jnp.concatenate([-sin, zero], axis=1), (1, H))
    sin_b = jnp.tile(jnp.concatenate([zero, sin], axis=1), (1, H))
    log_g = jnp.log(1.0 - jnp.exp2(-5.0 - jnp.arange(H, dtype=F32)))
    pos = jnp.arange(C, dtype=F32)
    diff = pos[:, None] - pos[None, :]
    dmat = jnp.where(diff[None] >= 0, jnp.exp(jnp.maximum(diff, 0.0)[None] * log_g[:, None, None]), 0.0)
    xi = jnp.exp((pos + 1.0)[None] * log_g[:, None])
    zeta = jnp.exp((C - 1.0 - pos)[None] * log_g[:, None])
    xi_f = jnp.repeat(xi.T, RET_DV, axis=1)
    zeta_f = jnp.repeat(zeta.T, RET_DK, axis=1)
    g_chunk = tuple(float(np.exp(C * np.log(1.0 - 2.0 ** (-5.0 - h)))) for h in range(H))
    return (cos_f, sin_a, sin_b, dmat, xi_f, zeta_f), g_chunk


def _retention(u3, tables, g_chunk):
    B, S, _ = u3.shape
    C = RET_CHUNK
    hdk, hdv = RET_HEADS * RET_DK, RET_HEADS * RET_DV
    const = lambda shape: pl.BlockSpec(shape, lambda b, n: (0,) * len(shape))
    return pl.pallas_call(
        functools.partial(_retention_kernel, g_chunk=g_chunk),
        grid=(B, S // C),
        in_specs=[pl.BlockSpec((1, C, hdk), lambda b, n: (b, n, 2)),
                  pl.BlockSpec((1, C, hdk), lambda b, n: (b, n, 3)),
                  pl.BlockSpec((1, C, hdv), lambda b, n: (b, n, 2)),
                  pl.BlockSpec((1, C, hdv), lambda b, n: (b, n, 3)),
                  pl.BlockSpec((C, hdk), lambda b, n: (n, 0)),
                  pl.BlockSpec((C, hdk), lambda b, n: (n, 0)),
                  pl.BlockSpec((C, hdk), lambda b, n: (n, 0)),
                  const((RET_HEADS, C, C)), const((C, hdv)), const((C, hdk))],
        out_specs=pl.BlockSpec((1, C, hdv), lambda b, n: (b, n, 0)),
        out_shape=jax.ShapeDtypeStruct((B, S, hdv), BF16),
        scratch_shapes=[pltpu.VMEM((RET_HEADS, RET_DK, RET_DV), F32)],
        compiler_params=_cparams("parallel", "arbitrary"),
        name="retention",
    )(u3, u3, u3, u3, *tables)


def _merge_kernel(x_ref, c_ref, og_ref, ga_ref, gb_ref, wc_ref, wr_ref, wo_ref, o_ref):
    yc = jnp.dot(c_ref[...], wc_ref[...], preferred_element_type=F32)
    yr = jnp.dot(og_ref[...], wr_ref[...], preferred_element_type=F32)
    merged = (jax.nn.sigmoid(ga_ref[...].astype(F32)) * yc + jax.nn.sigmoid(gb_ref[...].astype(F32)) * yr)
    o_ref[...] = x_ref[...] + jnp.dot(merged.astype(BF16), wo_ref[...], preferred_element_type=F32)


def _merge(x, c, og, u, wc, wr, wo, tm=512):
    T, D = x.shape
    cc = c.shape[1]
    tm = min(tm, T)
    gate_blk = u.shape[1] // D
    return pl.pallas_call(
        _merge_kernel,
        grid=(T // tm,),
        in_specs=[pl.BlockSpec((tm, D), lambda i: (i, 0)),
                  pl.BlockSpec((tm, cc), lambda i: (i, 0)),
                  pl.BlockSpec((tm, D), lambda i: (i, 0)),
                  pl.BlockSpec((tm, D), lambda i: (i, gate_blk - 2)),
                  pl.BlockSpec((tm, D), lambda i: (i, gate_blk - 1)),
                  pl.BlockSpec((cc, D), lambda i: (0, 0)),
                  pl.BlockSpec((D, D), lambda i: (0, 0)),
                  pl.BlockSpec((D, D), lambda i: (0, 0))],
        out_specs=pl.BlockSpec((tm, D), lambda i: (i, 0)),
        out_shape=jax.ShapeDtypeStruct((T, D), F32),
        compiler_params=_cparams("parallel"),
        name="merge_out",
    )(x, c, og, u, u, wc, wr, wo)


def _dense_ffn_kernel(x_ref, g_ref, wg_ref, wu_ref, wd_ref, o_ref, h_ref, acc_ref):
    f = pl.program_id(1)

    @pl.when(f == 0)
    def _():
        h_ref[...] = _rms(x_ref[...], g_ref[...]).astype(BF16)
        acc_ref[...] = x_ref[...]

    h = h_ref[...]
    a = _silu(jnp.dot(h, wg_ref[...], preferred_element_type=F32)) * jnp.dot(h, wu_ref[...], preferred_element_type=F32)
    acc_ref[...] += jnp.dot(a.astype(BF16), wd_ref[...], preferred_element_type=F32)

    @pl.when(f == pl.num_programs(1) - 1)
    def _():
        o_ref[...] = acc_ref[...]


def _dense_ffn(x, g, wg, wu, wd, tm=512, tf=1408):
    T, D = x.shape
    F = wg.shape[1]
    tm = min(tm, T)
    return pl.pallas_call(
        _dense_ffn_kernel,
        grid=(T // tm, F // tf),
        in_specs=[pl.BlockSpec((tm, D), lambda i, f: (i, 0)),
                  pl.BlockSpec((1, D), lambda i, f: (0, 0)),
                  pl.BlockSpec((D, tf), lambda i, f: (0, f)),
                  pl.BlockSpec((D, tf), lambda i, f: (0, f)),
                  pl.BlockSpec((tf, D), lambda i, f: (f, 0))],
        out_specs=pl.BlockSpec((tm, D), lambda i, f: (i, 0)),
        out_shape=jax.ShapeDtypeStruct((T, D), F32),
        scratch_shapes=[pltpu.VMEM((tm, D), BF16), pltpu.VMEM((tm, D), F32)],
        compiler_params=_cparams("parallel", "arbitrary"),
        name="dense_ffn",
    )(x, g.reshape(1, D), wg, wu, wd)


def _router_kernel(x_ref, g_ref, wr_ref, h_ref, mi_ref, mf_ref, cnt_ref, run_ref):
    @pl.when(pl.program_id(0) == 0)
    def _():
        run_ref[...] = jnp.zeros_like(run_ref)

    h = _rms(x_ref[...], g_ref[...])
    h_ref[...] = h.astype(BF16)
    logits = jnp.dot(h, wr_ref[...], preferred_element_type=F32, precision=lax.Precision.HIGHEST)
    tt = logits.shape[0]
    lane = lax.broadcasted_iota(jnp.int32, (tt, LANES), 1)
    neg = jnp.float32(-jnp.inf)
    logits = jnp.where(lane < N_EXPERTS, logits, neg)
    m1 = jnp.max(logits, axis=1, keepdims=True)
    i1 = jnp.min(jnp.where(logits == m1, lane, LANES), axis=1, keepdims=True)
    mask1 = lane == i1
    rest = jnp.where(mask1, neg, logits)
    m2 = jnp.max(rest, axis=1, keepdims=True)
    i2 = jnp.min(jnp.where(rest == m2, lane, LANES), axis=1, keepdims=True)
    mask2 = lane == i2
    d = jnp.exp(m2 - m1)
    g1 = 1.0 / (1.0 + d)
    g2 = d / (1.0 + d)
    onehot = jnp.where(mask1 | mask2, 1.0, 0.0)
    row = lax.broadcasted_iota(jnp.int32, (tt, tt), 0)
    col = lax.broadcasted_iota(jnp.int32, (tt, tt), 1)
    lower = jnp.where(col < row, 1.0, 0.0).astype(BF16)
    excl = jnp.dot(lower, onehot.astype(BF16), preferred_element_type=F32) + run_ref[...]
    r1 = jnp.sum(jnp.where(mask1, excl, 0.0), axis=1, keepdims=True).astype(jnp.int32)
    r2 = jnp.sum(jnp.where(mask2, excl, 0.0), axis=1, keepdims=True).astype(jnp.int32)
    run_ref[...] += jnp.sum(onehot, axis=0, keepdims=True)
    cnt_ref[0] = run_ref[...].astype(jnp.int32)
    mi_ref[...] = jnp.where(lane == 0, i1, jnp.where(lane == 1, i2, jnp.where(lane == 2, r1, jnp.where(lane == 3, r2, 0))))
    mf_ref[...] = jnp.where(lane == 0, g1, jnp.where(lane == 1, g2, 0.0))


def _router(x, g, w_router):
    T, D = x.shape
    nt = T // TOK_TILE
    wr = jnp.zeros((D, LANES), F32).at[:, :N_EXPERTS].set(w_router)
    return pl.pallas_call(
        _router_kernel,
        grid=(nt,),
        in_specs=[pl.BlockSpec((TOK_TILE, D), lambda s: (s, 0)),
                  pl.BlockSpec((1, D), lambda s: (0, 0)),
                  pl.BlockSpec((D, LANES), lambda s: (0, 0))],
        out_specs=[pl.BlockSpec((TOK_TILE, D), lambda s: (s, 0)),
                   pl.BlockSpec((TOK_TILE, LANES), lambda s: (s, 0)),
                   pl.BlockSpec((TOK_TILE, LANES), lambda s: (s, 0)),
                   pl.BlockSpec((1, 1, LANES), lambda s: (s, 0, 0))],
        out_shape=[jax.ShapeDtypeStruct((T, D), BF16),
                   jax.ShapeDtypeStruct((T, LANES), jnp.int32),
                   jax.ShapeDtypeStruct((T, LANES), F32),
                   jax.ShapeDtypeStruct((nt, 1, LANES), jnp.int32)],
        scratch_shapes=[pltpu.VMEM((1, LANES), F32)],
        compiler_params=_cparams("arbitrary"),
        name="router",
    )(x, g.reshape(1, D), wr)


def _dispatch_kernel(blk_ref, woff_ref, flag_ref, h_ref, d1_ref, d2_ref, g1_ref, g2_ref,
                     xin_ref, rg_ref, acc_ref, rga_ref, *, nt):
    i = (pl.program_id(0) * nt + pl.program_id(1)) * 2 + pl.program_id(2)
    flags = flag_ref[i]

    @pl.when((flags & 2) != 0)
    def _():
        acc_ref[...] = jnp.zeros_like(acc_ref)
        rga_ref[...] = jnp.zeros_like(rga_ref)

    @pl.when((flags & 1) != 0)
    def _():
        woff = pl.multiple_of(woff_ref[i], ROW_WIN)
        base = blk_ref[i] * ROW_BLK + woff
        r = lax.broadcasted_iota(jnp.int32, (ROW_WIN, TOK_TILE), 0)
        p1 = (d1_ref[0] - base) == r
        p2 = (d2_ref[0] - base) == r
        sel = jnp.where(p1 | p2, 1.0, 0.0).astype(BF16)
        acc_ref[pl.ds(woff, ROW_WIN), :] += jnp.dot(sel, h_ref[...], preferred_element_type=F32)
        rgv = jnp.sum(jnp.where(p1, g1_ref[0], 0.0) + jnp.where(p2, g2_ref[0], 0.0), axis=1, keepdims=True)
        rga_ref[pl.ds(woff, ROW_WIN), :] += jnp.broadcast_to(rgv, (ROW_WIN, LANES))

    @pl.when((flags & 4) != 0)
    def _():
        xin_ref[...] = acc_ref[...].astype(BF16)
        rg_ref[...] = rga_ref[...]


def _dispatch(h2, d1, d2, g1, g2, blk, woff, flags, n_blocks):
    T, D = h2.shape
    nt = T // TOK_TILE
    row3 = lambda a: a.reshape(nt, 1, TOK_TILE)
    tok_spec = pl.BlockSpec((1, 1, TOK_TILE), lambda e, s, v, *_: (s, 0, 0))
    out_map = lambda e, s, v, blk, woff, flags: (blk[(e * nt + s) * 2 + v], 0)
    return pl.pallas_call(
        functools.partial(_dispatch_kernel, nt=nt),
        grid_spec=pltpu.PrefetchScalarGridSpec(
            num_scalar_prefetch=3,
            grid=(N_EXPERTS, nt, 2),
            in_specs=[pl.BlockSpec((TOK_TILE, D), lambda e, s, v, *_: (s, 0)),
                      tok_spec, tok_spec, tok_spec, tok_spec],
            out_specs=[pl.BlockSpec((ROW_BLK, D), out_map),
                       pl.BlockSpec((ROW_BLK, LANES), out_map)],
            scratch_shapes=[pltpu.VMEM((ROW_BLK, D), F32), pltpu.VMEM((ROW_BLK, LANES), F32)]),
        out_shape=[jax.ShapeDtypeStruct((n_blocks * ROW_BLK, D), BF16),
                   jax.ShapeDtypeStruct((n_blocks * ROW_BLK, LANES), F32)],
        compiler_params=_cparams("arbitrary", "arbitrary", "arbitrary"),
        name="moe_dispatch",
    )(blk, woff, flags, h2, row3(d1), row3(d2), row3(g1), row3(g2))


def _moe_ffn_kernel(be_ref, bv_ref, x_ref, wg_ref, wu_ref, wd_ref, rg_ref, o_ref, acc_ref):
    b, f = pl.program_id(0), pl.program_id(1)

    @pl.when(f == 0)
    def _():
        acc_ref[...] = jnp.zeros_like(acc_ref)

    @pl.when(bv_ref[b] != 0)
    def _():
        x = x_ref[...]
        a = _silu(jnp.dot(x, wg_ref[0], preferred_element_type=F32)) * jnp.dot(x, wu_ref[0], preferred_element_type=F32)
        acc_ref[...] += jnp.dot(a.astype(BF16), wd_ref[0], preferred_element_type=F32)

    @pl.when(f == pl.num_programs(1) - 1)
    def _():
        o_ref[...] = (acc_ref[...] * rg_ref[:, 0:1]).astype(BF16)


def _moe_ffn(xin, rg, wg, wu, wd, block_e, block_valid, tf=512):
    R, D = xin.shape
    F = wg.shape[2]
    nb, nf = R // ROW_BLK, F // tf
    row_map = lambda b, f, be, bv: (jnp.where(bv[b] != 0, b, 0), 0)
    fsel = lambda b, f, bv: jnp.where(bv[b] != 0, f, nf - 1)
    return pl.pallas_call(
        _moe_ffn_kernel,
        grid_spec=pltpu.PrefetchScalarGridSpec(
            num_scalar_prefetch=2,
            grid=(nb, nf),
            in_specs=[pl.BlockSpec((ROW_BLK, D), row_map),
                      pl.BlockSpec((1, D, tf), lambda b, f, be, bv: (be[b], 0, fsel(b, f, bv))),
                      pl.BlockSpec((1, D, tf), lambda b, f, be, bv: (be[b], 0, fsel(b, f, bv))),
                      pl.BlockSpec((1, tf, D), lambda b, f, be, bv: (be[b], fsel(b, f, bv), 0)),
                      pl.BlockSpec((ROW_BLK, LANES), row_map)],
            out_specs=pl.BlockSpec((ROW_BLK, D), lambda b, f, be, bv: (b, 0)),
            scratch_shapes=[pltpu.VMEM((ROW_BLK, D), F32)]),
        out_shape=jax.ShapeDtypeStruct((R, D), BF16),
        compiler_params=_cparams("arbitrary", "arbitrary"),
        name="moe_ffn",
    )(block_e, block_valid, xin, wg, wu, wd, rg)


def _combine_kernel(win_ref, valid_ref, x_ref, y_ref, d1_ref, d2_ref, o_ref, acc_ref):
    v = pl.program_id(1)
    i = pl.program_id(0) * COMBINE_VISITS + v

    @pl.when(v == 0)
    def _():
        acc_ref[...] = x_ref[...]

    @pl.when(valid_ref[i] != 0)
    def _():
        base = win_ref[i] * ROW_WIN
        r = lax.broadcasted_iota(jnp.int32, (TOK_TILE, ROW_WIN), 1)
        sel = ((d1_ref[...] - base) == r) | ((d2_ref[...] - base) == r)
        acc_ref[...] += jnp.dot(jnp.where(sel, 1.0, 0.0).astype(BF16), y_ref[...], preferred_element_type=F32)

    @pl.when(v == COMBINE_VISITS - 1)
    def _():
        o_ref[...] = acc_ref[...]


def _combine(x, y, d1, d2, win, valid):
    T, D = x.shape
    nt = T // TOK_TILE
    tok_spec = pl.BlockSpec((TOK_TILE, 1), lambda s, v, *_: (s, 0))
    return pl.pallas_call(
        _combine_kernel,
        grid_spec=pltpu.PrefetchScalarGridSpec(
            num_scalar_prefetch=2,
            grid=(nt, COMBINE_VISITS),
            in_specs=[pl.BlockSpec((TOK_TILE, D), lambda s, v, *_: (s, 0)),
                      pl.BlockSpec((ROW_WIN, D), lambda s, v, win, valid: (win[s * COMBINE_VISITS + v], 0)),
                      tok_spec, tok_spec],
            out_specs=pl.BlockSpec((TOK_TILE, D), lambda s, v, *_: (s, 0)),
            scratch_shapes=[pltpu.VMEM((TOK_TILE, D), F32)]),
        out_shape=jax.ShapeDtypeStruct((T, D), F32),
        compiler_params=_cparams("arbitrary", "arbitrary"),
        name="moe_combine",
    )(win, valid, x, y, d1.reshape(T, 1), d2.reshape(T, 1))


def _moe_layer(x, g, w_router, wg, wu, wd):
    T, D = x.shape
    nt = T // TOK_TILE
    n_blocks = (2 * T + N_EXPERTS * (ROW_BLK - 1) + ROW_BLK - 1) // ROW_BLK
    n_win = n_blocks * (ROW_BLK // ROW_WIN)
    h2, mi, mf, cnt = _router(x, g, w_router)
    e1, e2, r1, r2 = mi[:, 0], mi[:, 1], mi[:, 2], mi[:, 3]
    g1, g2 = mf[:, 0], mf[:, 1]
    cnt_incl = cnt[:, 0, :N_EXPERTS]
    cnt_excl = jnp.concatenate([jnp.zeros((1, N_EXPERTS), jnp.int32), cnt_incl[:-1]], axis=0)
    counts = cnt_incl[-1]
    padded = ((counts + ROW_BLK - 1) // ROW_BLK) * ROW_BLK
    pad_end = jnp.cumsum(padded)
    pad_start = pad_end - padded
    d1 = pad_start[e1] + r1
    d2 = pad_start[e2] + r2
    n_se = cnt_incl - cnt_excl
    start = pad_start[None, :] + cnt_excl
    w_a = start // ROW_WIN
    w_b = (start + n_se - 1) // ROW_WIN
    has = n_se > 0
    win = jnp.stack([w_a, w_b], axis=-1)
    valid = jnp.stack([has, has & (w_b != w_a)], axis=-1)
    wv = jnp.where(valid, win, -1).transpose(1, 0, 2).reshape(-1)
    carried = jnp.maximum(lax.cummax(wv, axis=0), 0)
    blk = carried // (ROW_BLK // ROW_WIN)
    woff = (carried % (ROW_BLK // ROW_WIN)) * ROW_WIN
    change = blk[1:] != blk[:-1]
    one = jnp.ones((1,), bool)
    flags = (valid.transpose(1, 0, 2).reshape(-1).astype(jnp.int32)
             + 2 * jnp.concatenate([one, change]).astype(jnp.int32)
             + 4 * jnp.concatenate([change, one]).astype(jnp.int32))
    xin, rg = _dispatch(h2, d1, d2, g1, g2, blk.astype(jnp.int32), woff.astype(jnp.int32), flags, n_blocks)
    bstart = jnp.arange(n_blocks, dtype=jnp.int32) * ROW_BLK
    block_e = jnp.minimum(jnp.searchsorted(pad_end, bstart, side='right'), N_EXPERTS - 1).astype(jnp.int32)
    block_valid = (bstart < pad_end[-1]).astype(jnp.int32)
    y = _moe_ffn(xin, rg, wg, wu, wd, block_e, block_valid)
    cwin = jnp.clip(jnp.where(valid, win, w_a[..., None]), 0, n_win - 1).reshape(-1).astype(jnp.int32)
    return _combine(x, y, d1, d2, cwin, valid.reshape(-1).astype(jnp.int32))


def _ple_kernel(x_ref, p_ref, g_ref, wgate_ref, wproj_ref, gf_ref, o_ref, *, final):
    x = x_ref[...]
    h = _rms(x, g_ref[...]).astype(BF16)
    gate = jax.nn.sigmoid(jnp.dot(h, wgate_ref[...], preferred_element_type=F32))
    proj = jnp.dot(p_ref[...].astype(BF16), wproj_ref[...], preferred_element_type=F32)
    y = x + gate * proj
    o_ref[...] = _rms(y, gf_ref[...]) if final else y


def _ple(x, p, g, wgate, wproj, g_final, final, tm=512):
    T, D = x.shape
    P = p.shape[1]
    tm = min(tm, T)
    return pl.pallas_call(
        functools.partial(_ple_kernel, final=final),
        grid=(T // tm,),
        in_specs=[pl.BlockSpec((tm, D), lambda i: (i, 0)),
                  pl.BlockSpec((tm, P), lambda i: (i, 0)),
                  pl.BlockSpec((1, D), lambda i: (0, 0)),
                  pl.BlockSpec((D, D), lambda i: (0, 0)),
                  pl.BlockSpec((P, D), lambda i: (0, 0)),
                  pl.BlockSpec((1, D), lambda i: (0, 0))],
        out_specs=pl.BlockSpec((tm, D), lambda i: (i, 0)),
        out_shape=jax.ShapeDtypeStruct((T, D), F32),
        compiler_params=_cparams("parallel"),
        name="ple",
    )(x, p, g.reshape(1, D), wgate, wproj, g_final.reshape(1, D))


def kernel(x, p, g_mix, w_in, conv_w, conv_b, conv_ln_g, conv_ln_b, w_conv_out, w_ret_out, w_out, g_ffn, w_dense_gate, w_dense_up, w_dense_down, w_router, w_exp_gate, w_exp_up, w_exp_down, g_ple, w_ple_gate, w_ple_proj, g_final):
    B, S, D = x.shape
    T = B * S
    depth = w_in.shape[0]
    tables, g_chunk = _retention_tables(S)
    bf = lambda w: w.astype(BF16)
    xf = x.reshape(T, D)
    for i in range(depth):
        u = _inproj(xf, g_mix[i], bf(w_in[i]))
        u3 = u.reshape(B, S, u.shape[1])
        c = _conv_branch(u3, conv_w[i], conv_b[i], conv_ln_g[i], conv_ln_b[i])
        og = _retention(u3, tables, g_chunk)
        xf = _merge(xf, c.reshape(T, -1), og.reshape(T, -1), u, bf(w_conv_out[i]), bf(w_ret_out[i]), bf(w_out[i]))
        j = i // 2
        if i % 2 == 0:
            xf = _dense_ffn(xf, g_ffn[i], bf(w_dense_gate[j]), bf(w_dense_up[j]), bf(w_dense_down[j]))
        else:
            xf = _moe_layer(xf, g_ffn[i], w_router[j], bf(w_exp_gate[j]), bf(w_exp_up[j]), bf(w_exp_down[j]))
        xf = _ple(xf, p[i].reshape(T, -1), g_ple[i], bf(w_ple_gate[i]), bf(w_ple_proj[i]), g_final, i == depth - 1)
    return xf.reshape(B, S, D)
```

```python
import functools

import numpy as np
import jax
import jax.numpy as jnp
from jax import lax
from jax.experimental import pallas as pl
from jax.experimental.pallas import tpu as pltpu

F32 = jnp.float32
BF16 = jnp.bfloat16

EPS = 1e-6
CONV_WIDTH = 31
RET_HEADS = 8
RET_DK = 64
RET_DV = 128
RET_CHUNK = 128
ROPE_BASE = 10000.0
N_EXPERTS = 8

V7X_VMEM_LIMIT_BYTES = 56 * 1024 * 1024
LANES = 128
SUBLANES = 8
CONV_HALO = 32
TOK_TILE = 256
ROW_BLK = 512


def _cparams(*sem):
    return pltpu.CompilerParams(dimension_semantics=sem, vmem_limit_bytes=V7X_VMEM_LIMIT_BYTES)


def _rms(x, g):
    return x * lax.rsqrt(jnp.mean(x * x, axis=-1, keepdims=True) + EPS) * g


def _silu(x):
    return x * jax.nn.sigmoid(x)


def _inproj_kernel(x_ref, g_ref, w_ref, o_ref, h_ref):
    @pl.when(pl.program_id(1) == 0)
    def _():
        h_ref[...] = _rms(x_ref[...], g_ref[...]).astype(BF16)

    o_ref[...] = jnp.dot(h_ref[...], w_ref[...], preferred_element_type=F32).astype(BF16)


def _inproj(x, g, w, tm=1024, tn=1024):
    T, D = x.shape
    N = w.shape[1]
    tm = min(tm, T)
    return pl.pallas_call(
        _inproj_kernel,
        grid=(T // tm, N // tn),
        in_specs=[pl.BlockSpec((tm, D), lambda i, j: (i, 0)),
                  pl.BlockSpec((1, D), lambda i, j: (0, 0)),
                  pl.BlockSpec((D, tn), lambda i, j: (0, j))],
        out_specs=pl.BlockSpec((tm, tn), lambda i, j: (i, j)),
        out_shape=jax.ShapeDtypeStruct((T, N), BF16),
        scratch_shapes=[pltpu.VMEM((tm, D), BF16)],
        compiler_params=_cparams("parallel", "arbitrary"),
        name="inproj",
    )(x, g.reshape(1, D), w)


def _conv_kernel(cur_ref, halo_ref, w_ref, b_ref, lg_ref, lb_ref, o_ref, zbuf, cbuf, *, ts, cc, rc):
    cur = cur_ref[0].astype(F32)
    zbuf[CONV_HALO:CONV_HALO + ts, :] = cur[:, :cc] * jax.nn.sigmoid(cur[:, cc:])
    halo = halo_ref[0].astype(F32)
    zh = halo[:, :cc] * jax.nn.sigmoid(halo[:, cc:])
    zbuf[:CONV_HALO, :] = jnp.where(pl.program_id(1) == 0, 0.0, zh)
    zbuf[CONV_HALO + ts:, :] = jnp.zeros((2 * SUBLANES, cc), F32)
    first = CONV_HALO - (CONV_WIDTH - 1)
    for c in range(cc // LANES):
        cs = slice(c * LANES, (c + 1) * LANES)
        for r in range(ts // rc):
            acc = jnp.broadcast_to(b_ref[:, cs], (rc, LANES))
            for b in range(SUBLANES):
                part = None
                for a in range((first + CONV_WIDTH - 1) // SUBLANES + 1):
                    j = SUBLANES * a + b - first
                    if 0 <= j < CONV_WIDTH:
                        lo = r * rc + SUBLANES * a
                        term = w_ref[j:j + 1, cs] * zbuf[lo:lo + rc + SUBLANES, cs]
                        part = term if part is None else part + term
                acc = acc + part[b:b + rc]
            cbuf[r * rc:(r + 1) * rc, cs] = acc
    z = cbuf[...]
    mu = jnp.mean(z, axis=-1, keepdims=True)
    var = jnp.mean(jnp.square(z - mu), axis=-1, keepdims=True)
    y = (z - mu) * lax.rsqrt(var + EPS) * lg_ref[...] + lb_ref[...]
    o_ref[0] = _silu(y).astype(BF16)


def _conv_branch(u3, conv_w, conv_b, ln_g, ln_b, ts=256, rc=64):
    B, S, _ = u3.shape
    cc = conv_w.shape[1]
    ts = min(ts, S)
    hb = ts // CONV_HALO
    return pl.pallas_call(
        functools.partial(_conv_kernel, ts=ts, cc=cc, rc=rc),
        grid=(B, S // ts),
        in_specs=[pl.BlockSpec((1, ts, 2 * cc), lambda b, i: (b, i, 0)),
                  pl.BlockSpec((1, CONV_HALO, 2 * cc), lambda b, i: (b, jnp.maximum(i * hb - 1, 0), 0)),
                  pl.BlockSpec((CONV_WIDTH, cc), lambda b, i: (0, 0)),
                  pl.BlockSpec((1, cc), lambda b, i: (0, 0)),
                  pl.BlockSpec((1, cc), lambda b, i: (0, 0)),
                  pl.BlockSpec((1, cc), lambda b, i: (0, 0))],
        out_specs=pl.BlockSpec((1, ts, cc), lambda b, i: (b, i, 0)),
        out_shape=jax.ShapeDtypeStruct((B, S, cc), BF16),
        scratch_shapes=[pltpu.VMEM((ts + CONV_HALO + 2 * SUBLANES, cc), F32), pltpu.VMEM((ts, cc), F32)],
        compiler_params=_cparams("parallel", "parallel"),
        name="conv_branch",
    )(u3, u3, conv_w, conv_b.reshape(1, cc), ln_g.reshape(1, cc), ln_b.reshape(1, cc))


def _retention_kernel(q_ref, k_ref, v_ref, gr_ref, cos_ref, sa_ref, sb_ref, dmat_ref, xi_ref, zeta_ref,
                      o_ref, state, *, g_chunk):
    @pl.when(pl.program_id(1) == 0)
    def _():
        state[...] = jnp.zeros_like(state)

    hdk = RET_HEADS * RET_DK
    cos, sa, sb = cos_ref[...], sa_ref[...], sb_ref[...]

    def rope(t):
        return t * cos + pltpu.roll(t, hdk - RET_DK // 2, 1) * sa + pltpu.roll(t, RET_DK // 2, 1) * sb

    q = (rope(q_ref[0].astype(F32)) * (RET_DK ** -0.5)).astype(BF16)
    k = rope(k_ref[0].astype(F32))
    kz = (k * zeta_ref[...]).astype(BF16)
    k = k.astype(BF16)
    v = v_ref[0]
    for h in range(RET_HEADS):
        ks = slice(h * RET_DK, (h + 1) * RET_DK)
        vs = slice(h * RET_DV, (h + 1) * RET_DV)
        qh, kh, vh = q[:, ks], k[:, ks], v[:, vs]
        s = lax.dot_general(qh, kh, (((1,), (1,)), ((), ())), preferred_element_type=F32) * dmat_ref[h]
        intra = jnp.dot(s.astype(BF16), vh, preferred_element_type=F32)
        st = state[h]
        cross = jnp.dot(qh, st.astype(BF16), preferred_element_type=F32) * xi_ref[:, vs]
        state[h] = g_chunk[h] * st + lax.dot_general(kz[:, ks], vh, (((0,), (0,)), ((), ())),
                                                     preferred_element_type=F32)
        o = intra + cross
        mu = jnp.mean(o, axis=-1, keepdims=True)
        var = jnp.mean(jnp.square(o - mu), axis=-1, keepdims=True)
        o = (o - mu) * lax.rsqrt(var + EPS)
        o_ref[0, :, vs] = (_silu(gr_ref[0, :, vs].astype(F32)) * o).astype(BF16)


def _retention_tables(S):
    C, H = RET_CHUNK, RET_HEADS
    inv_freq = ROPE_BASE ** (-jnp.arange(0, RET_DK, 2, dtype=F32) / RET_DK)
    ang = jnp.arange(S, dtype=F32)[:, None] * inv_freq[None, :]
    cos, sin = jnp.cos(ang), jnp.sin(ang)
    zero = jnp.zeros_like(sin)
    cos_f = jnp.tile(cos, (1, 2 * H))
    sin_a = jnp.tile(jnp.concatenate([-sin, zero], axis=1), (1, H))
    sin_b = jnp.tile(jnp.concatenate([zero, sin], axis=1), (1, H))
    log_g = jnp.log(1.0 - jnp.exp2(-5.0 - jnp.arange(H, dtype=F32)))
    pos = jnp.arange(C, dtype=F32)
    diff = pos[:, None] - pos[None, :]
    dmat = jnp.where(diff[None] >= 0, jnp.exp(jnp.maximum(diff, 0.0)[None] * log_g[:, None, None]), 0.0)
    xi = jnp.exp((pos + 1.0)[None] * log_g[:, None])
    zeta = jnp.exp((C - 1.0 - pos)[None] * log_g[:, None])
    xi_f = jnp.repeat(xi.T, RET_DV, axis=1)
    zeta_f = jnp.repeat(zeta.T, RET_DK, axis=1)
    g_chunk = tuple(float(np.exp(C * np.log(1.0 - 2.0 ** (-5.0 - h)))) for h in range(H))
    return (cos_f, sin_a, sin_b, dmat, xi_f, zeta_f), g_chunk


def _retention(u3, tables, g_chunk):
    B, S, _ = u3.shape
    C = RET_CHUNK
    hdk, hdv = RET_HEADS * RET_DK, RET_HEADS * RET_DV
    const = lambda shape: pl.BlockSpec(shape, lambda b, n: (0,) * len(shape))
    return pl.pallas_call(
        functools.partial(_retention_kernel, g_chunk=g_chunk),
        grid=(B, S // C),
        in_specs=[pl.BlockSpec((1, C, hdk), lambda b, n: (b, n, 2)),
                  pl.BlockSpec((1, C, hdk), lambda b, n: (b, n, 3)),
                  pl.BlockSpec((1, C, hdv), lambda b, n: (b, n, 2)),
                  pl.BlockSpec((1, C, hdv), lambda b, n: (b, n, 3)),
                  pl.BlockSpec((C, hdk), lambda b, n: (n, 0)),
                  pl.BlockSpec((C, hdk), lambda b, n: (n, 0)),
                  pl.BlockSpec((C, hdk), lambda b, n: (n, 0)),
                  const((RET_HEADS, C, C)), const((C, hdv)), const((C, hdk))],
        out_specs=pl.BlockSpec((1, C, hdv), lambda b, n: (b, n, 0)),
        out_shape=jax.ShapeDtypeStruct((B, S, hdv), BF16),
        scratch_shapes=[pltpu.VMEM((RET_HEADS, RET_DK, RET_DV), F32)],
        compiler_params=_cparams("parallel", "arbitrary"),
        name="retention",
    )(u3, u3, u3, u3, *tables)


def _merge_kernel(x_ref, c_ref, og_ref, ga_ref, gb_ref, wc_ref, wr_ref, wo_ref, o_ref):
    yc = jnp.dot(c_ref[...], wc_ref[...], preferred_element_type=F32)
    yr = jnp.dot(og_ref[...], wr_ref[...], preferred_element_type=F32)
    merged = (jax.nn.sigmoid(ga_ref[...].astype(F32)) * yc + jax.nn.sigmoid(gb_ref[...].astype(F32)) * yr)
    o_ref[...] = x_ref[...] + jnp.dot(merged.astype(BF16), wo_ref[...], preferred_element_type=F32)


def _merge(x, c, og, u, wc, wr, wo, tm=512):
    T, D = x.shape
    cc = c.shape[1]
    tm = min(tm, T)
    gate_blk = u.shape[1] // D
    return pl.pallas_call(
        _merge_kernel,
        grid=(T // tm,),
        in_specs=[pl.BlockSpec((tm, D), lambda i: (i, 0)),
                  pl.BlockSpec((tm, cc), lambda i: (i, 0)),
                  pl.BlockSpec((tm, D), lambda i: (i, 0)),
                  pl.BlockSpec((tm, D), lambda i: (i, gate_blk - 2)),
                  pl.BlockSpec((tm, D), lambda i: (i, gate_blk - 1)),
                  pl.BlockSpec((cc, D), lambda i: (0, 0)),
                  pl.BlockSpec((D, D), lambda i: (0, 0)),
                  pl.BlockSpec((D, D), lambda i: (0, 0))],
        out_specs=pl.BlockSpec((tm, D), lambda i: (i, 0)),
        out_shape=jax.ShapeDtypeStruct((T, D), F32),
        compiler_params=_cparams("parallel"),
        name="merge_out",
    )(x, c, og, u, u, wc, wr, wo)


def _dense_ffn_kernel(x_ref, g_ref, wg_ref, wu_ref, wd_ref, o_ref, h_ref):
    f = pl.program_id(1)

    @pl.when(f == 0)
    def _():
        h_ref[...] = _rms(x_ref[...], g_ref[...]).astype(BF16)
        o_ref[...] = x_ref[...]

    h = h_ref[...]
    a = _silu(jnp.dot(h, wg_ref[...], preferred_element_type=F32)) * jnp.dot(h, wu_ref[...], preferred_element_type=F32)
    o_ref[...] += jnp.dot(a.astype(BF16), wd_ref[...], preferred_element_type=F32)


def _dense_ffn(x, g, wg, wu, wd, tm=512, tf=1408):
    T, D = x.shape
    F = wg.shape[1]
    tm = min(tm, T)
    return pl.pallas_call(
        _dense_ffn_kernel,
        grid=(T // tm, F // tf),
        in_specs=[pl.BlockSpec((tm, D), lambda i, f: (i, 0)),
                  pl.BlockSpec((1, D), lambda i, f: (0, 0)),
                  pl.BlockSpec((D, tf), lambda i, f: (0, f)),
                  pl.BlockSpec((D, tf), lambda i, f: (0, f)),
                  pl.BlockSpec((tf, D), lambda i, f: (f, 0))],
        out_specs=pl.BlockSpec((tm, D), lambda i, f: (i, 0)),
        out_shape=jax.ShapeDtypeStruct((T, D), F32),
        scratch_shapes=[pltpu.VMEM((tm, D), BF16)],
        compiler_params=_cparams("parallel", "arbitrary"),
        name="dense_ffn",
    )(x, g.reshape(1, D), wg, wu, wd)


META_ROWS = 8


def _router_kernel(x_ref, g_ref, wr_ref, meta_ref, cnt_ref, run_ref):
    @pl.when(pl.program_id(0) == 0)
    def _():
        run_ref[...] = jnp.zeros_like(run_ref)

    h = _rms(x_ref[...], g_ref[...])
    logits = jnp.dot(h, wr_ref[...], preferred_element_type=F32, precision=lax.Precision.HIGHEST)
    tt = logits.shape[0]
    lane = lax.broadcasted_iota(jnp.int32, (tt, LANES), 1)
    neg = jnp.float32(-jnp.inf)
    logits = jnp.where(lane < N_EXPERTS, logits, neg)
    m1 = jnp.max(logits, axis=1, keepdims=True)
    i1 = jnp.min(jnp.where(logits == m1, lane, LANES), axis=1, keepdims=True)
    mask1 = lane == i1
    rest = jnp.where(mask1, neg, logits)
    m2 = jnp.max(rest, axis=1, keepdims=True)
    i2 = jnp.min(jnp.where(rest == m2, lane, LANES), axis=1, keepdims=True)
    mask2 = lane == i2
    d = jnp.exp(m2 - m1)
    g1 = 1.0 / (1.0 + d)
    g2 = d / (1.0 + d)
    onehot = jnp.where(mask1 | mask2, 1.0, 0.0)
    row = lax.broadcasted_iota(jnp.int32, (tt, tt), 0)
    col = lax.broadcasted_iota(jnp.int32, (tt, tt), 1)
    lower = jnp.where(col < row, 1.0, 0.0).astype(BF16)
    excl = jnp.dot(lower, onehot.astype(BF16), preferred_element_type=F32) + run_ref[...]
    r1 = jnp.sum(jnp.where(mask1, excl, 0.0), axis=1, keepdims=True).astype(jnp.int32)
    r2 = jnp.sum(jnp.where(mask2, excl, 0.0), axis=1, keepdims=True).astype(jnp.int32)
    run_ref[...] += jnp.sum(onehot, axis=0, keepdims=True)
    cnt_ref[0] = run_ref[...].astype(jnp.int32)
    gb1 = pltpu.bitcast(jnp.broadcast_to(g1, (tt, LANES)), jnp.int32)
    gb2 = pltpu.bitcast(jnp.broadcast_to(g2, (tt, LANES)), jnp.int32)
    rec = jnp.where(lane == 0, i1, jnp.where(lane == 1, i2, jnp.where(lane == 2, r1, jnp.where(
        lane == 3, r2, jnp.where(lane == 4, gb1, jnp.where(lane == 5, gb2, 0))))))
    meta_ref[0] = rec.T[:META_ROWS, :]


def _router(x, g, w_router):
    T, D = x.shape
    nt = T // TOK_TILE
    wr = jnp.zeros((D, LANES), F32).at[:, :N_EXPERTS].set(w_router)
    return pl.pallas_call(
        _router_kernel,
        grid=(nt,),
        in_specs=[pl.BlockSpec((TOK_TILE, D), lambda s: (s, 0)),
                  pl.BlockSpec((1, D), lambda s: (0, 0)),
                  pl.BlockSpec((D, LANES), lambda s: (0, 0))],
        out_specs=[pl.BlockSpec((1, META_ROWS, TOK_TILE), lambda s: (s, 0, 0)),
                   pl.BlockSpec((1, 1, LANES), lambda s: (s, 0, 0))],
        out_shape=[jax.ShapeDtypeStruct((nt, META_ROWS, TOK_TILE), jnp.int32),
                   jax.ShapeDtypeStruct((nt, 1, LANES), jnp.int32)],
        scratch_shapes=[pltpu.VMEM((1, LANES), F32)],
        compiler_params=_cparams("arbitrary"),
        name="router",
    )(x, g.reshape(1, D), wr)


def _dispatch_kernel(pad_ref, dest_ref, x_ref, g_ref, xin_hbm, hbuf, sem, zsem, *, nt):
    s = pl.program_id(0)
    slot = s % 2
    hbuf[slot] = _rms(x_ref[...], g_ref[...])

    def issue(t, c):
        for k in range(2):
            pltpu.make_async_copy(hbuf.at[slot, pl.ds(t, 1)],
                                  xin_hbm.at[pl.ds(dest_ref[0, 0, k * TOK_TILE + t], 1)],
                                  sem.at[slot]).start()
        return c

    lax.fori_loop(0, TOK_TILE, issue, 0, unroll=8)

    def drain(sl):
        for _ in range(2):
            pltpu.make_async_copy(hbuf.at[sl], xin_hbm.at[pl.ds(0, TOK_TILE)], sem.at[sl]).wait()

    @pl.when(s > 0)
    def _():
        drain(1 - slot)

    @pl.when(s == nt - 1)
    def _():
        drain(slot)
        hbuf[0] = jnp.zeros((TOK_TILE, hbuf.shape[2]), F32)

        def zissue(r, c):
            pltpu.make_async_copy(hbuf.at[0, pl.ds(0, 1)], xin_hbm.at[pl.ds(r, 1)], zsem).start()
            return c

        def zwait(r, c):
            pltpu.make_async_copy(hbuf.at[0, pl.ds(0, 1)], xin_hbm.at[pl.ds(0, 1)], zsem).wait()
            return c

        for e in range(N_EXPERTS):
            lo, hi = pad_ref[e], pad_ref[N_EXPERTS + e]
            lax.fori_loop(lo, hi, zissue, 0)
            lax.fori_loop(lo, hi, zwait, 0)

        def tissue(r, c):
            pltpu.make_async_copy(hbuf.at[0], xin_hbm.at[pl.ds(pl.multiple_of(r * TOK_TILE, TOK_TILE), TOK_TILE)],
                                  zsem).start()
            return c

        def twait(r, c):
            pltpu.make_async_copy(hbuf.at[0], xin_hbm.at[pl.ds(0, TOK_TILE)], zsem).wait()
            return c

        tail_lo, tail_hi = pad_ref[2 * N_EXPERTS - 1] // TOK_TILE, xin_hbm.shape[0] // TOK_TILE
        lax.fori_loop(tail_lo, tail_hi, tissue, 0)
        lax.fori_loop(tail_lo, tail_hi, twait, 0)


def _dispatch(x, g, dest, pad_info, n_rows):
    T, D = x.shape
    nt = T // TOK_TILE
    return pl.pallas_call(
        functools.partial(_dispatch_kernel, nt=nt),
        grid_spec=pltpu.PrefetchScalarGridSpec(
            num_scalar_prefetch=1,
            grid=(nt,),
            in_specs=[pl.BlockSpec((1, 1, 2 * TOK_TILE), lambda s, pad: (s, 0, 0), memory_space=pltpu.SMEM),
                      pl.BlockSpec((TOK_TILE, D), lambda s, pad: (s, 0)),
                      pl.BlockSpec((1, D), lambda s, pad: (0, 0))],
            out_specs=pl.BlockSpec(memory_space=pl.ANY),
            scratch_shapes=[pltpu.VMEM((2, TOK_TILE, D), F32),
                            pltpu.SemaphoreType.DMA((2,)), pltpu.SemaphoreType.DMA(())]),
        out_shape=jax.ShapeDtypeStruct((n_rows, D), F32),
        compiler_params=_cparams("arbitrary"),
        name="moe_dispatch",
    )(pad_info, dest, x, g.reshape(1, D))


def _moe_ffn_kernel(be_ref, bv_ref, x_ref, wg_ref, wu_ref, wd_ref, o_ref, xb_ref):
    b, f = pl.program_id(0), pl.program_id(1)

    @pl.when(f == 0)
    def _():
        xb_ref[...] = x_ref[...].astype(BF16)
        o_ref[...] = jnp.zeros_like(o_ref)

    @pl.when(bv_ref[b] != 0)
    def _():
        x = xb_ref[...]
        a = _silu(jnp.dot(x, wg_ref[0], preferred_element_type=F32)) * jnp.dot(x, wu_ref[0], preferred_element_type=F32)
        o_ref[...] += jnp.dot(a.astype(BF16), wd_ref[0], preferred_element_type=F32)


def _moe_ffn(xin, wg, wu, wd, block_e, block_valid, tf=512):
    R, D = xin.shape
    F = wg.shape[2]
    nb, nf = R // ROW_BLK, F // tf
    fsel = lambda b, f, bv: jnp.where(bv[b] != 0, f, nf - 1)
    return pl.pallas_call(
        _moe_ffn_kernel,
        grid_spec=pltpu.PrefetchScalarGridSpec(
            num_scalar_prefetch=2,
            grid=(nb, nf),
            in_specs=[pl.BlockSpec((ROW_BLK, D), lambda b, f, be, bv: (jnp.where(bv[b] != 0, b, 0), 0)),
                      pl.BlockSpec((1, D, tf), lambda b, f, be, bv: (be[b], 0, fsel(b, f, bv))),
                      pl.BlockSpec((1, D, tf), lambda b, f, be, bv: (be[b], 0, fsel(b, f, bv))),
                      pl.BlockSpec((1, tf, D), lambda b, f, be, bv: (be[b], fsel(b, f, bv), 0))],
            out_specs=pl.BlockSpec((ROW_BLK, D), lambda b, f, be, bv: (b, 0)),
            scratch_shapes=[pltpu.VMEM((ROW_BLK, D), BF16)]),
        out_shape=jax.ShapeDtypeStruct((R, D), F32),
        compiler_params=_cparams("arbitrary", "arbitrary"),
        name="moe_ffn",
    )(block_e, block_valid, xin, wg, wu, wd)


def _combine_kernel(dcur_ref, dnext_ref, x_ref, g1_ref, g2_ref, y_hbm, o_ref, ya, yb, sem, *, nt):
    s = pl.program_id(0)
    slot = s % 2

    def gather(dref, sl):
        def issue(t, c):
            pltpu.make_async_copy(y_hbm.at[pl.ds(dref[0, 0, t], 1)], ya.at[sl, pl.ds(t, 1)], sem.at[sl]).start()
            pltpu.make_async_copy(y_hbm.at[pl.ds(dref[0, 0, TOK_TILE + t], 1)], yb.at[sl, pl.ds(t, 1)],
                                  sem.at[sl]).start(priority=1)
            return c
        lax.fori_loop(0, TOK_TILE, issue, 0, unroll=8)

    @pl.when(s == 0)
    def _():
        gather(dcur_ref, slot)

    @pl.when(s + 1 < nt)
    def _():
        gather(dnext_ref, 1 - slot)

    for buf in (ya, yb):
        pltpu.make_async_copy(y_hbm.at[pl.ds(0, TOK_TILE)], buf.at[slot], sem.at[slot]).wait()
    o_ref[...] = x_ref[...] + g1_ref[...] * ya[slot] + g2_ref[...] * yb[slot]


def _combine(x, y, dest, g1, g2):
    T, D = x.shape
    nt = T // TOK_TILE
    smem_spec = lambda imap: pl.BlockSpec((1, 1, 2 * TOK_TILE), imap, memory_space=pltpu.SMEM)
    col_spec = pl.BlockSpec((TOK_TILE, 1), lambda s: (s, 0))
    return pl.pallas_call(
        functools.partial(_combine_kernel, nt=nt),
        grid=(nt,),
        in_specs=[smem_spec(lambda s: (s, 0, 0)),
                  smem_spec(lambda s: (jnp.minimum(s + 1, nt - 1), 0, 0)),
                  pl.BlockSpec((TOK_TILE, D), lambda s: (s, 0)),
                  col_spec, col_spec,
                  pl.BlockSpec(memory_space=pl.ANY)],
        out_specs=pl.BlockSpec((TOK_TILE, D), lambda s: (s, 0)),
        out_shape=jax.ShapeDtypeStruct((T, D), F32),
        scratch_shapes=[pltpu.VMEM((2, TOK_TILE, D), F32), pltpu.VMEM((2, TOK_TILE, D), F32),
                        pltpu.SemaphoreType.DMA((2,))],
        compiler_params=_cparams("arbitrary"),
        name="moe_combine",
    )(dest, dest, x, g1.reshape(T, 1), g2.reshape(T, 1), y)


def _moe_layer(x, g, w_router, wg, wu, wd):
    T, D = x.shape
    nt = T // TOK_TILE
    n_blocks = (2 * T + N_EXPERTS * (ROW_BLK - 1) + ROW_BLK - 1) // ROW_BLK
    meta, cnt = _router(x, g, w_router)
    e1, e2, r1, r2 = meta[:, 0], meta[:, 1], meta[:, 2], meta[:, 3]
    g1 = lax.bitcast_convert_type(meta[:, 4], F32)
    g2 = lax.bitcast_convert_type(meta[:, 5], F32)
    counts = cnt[-1, 0, :N_EXPERTS]
    padded = ((counts + ROW_BLK - 1) // ROW_BLK) * ROW_BLK
    pad_end = jnp.cumsum(padded)
    pad_start = pad_end - padded
    dest = jnp.concatenate([pad_start[e1] + r1, pad_start[e2] + r2], axis=1).reshape(nt, 1, 2 * TOK_TILE)
    pad_info = jnp.concatenate([pad_start + counts, pad_end]).astype(jnp.int32)
    xin = _dispatch(x, g, dest, pad_info, n_blocks * ROW_BLK)
    bstart = jnp.arange(n_blocks, dtype=jnp.int32) * ROW_BLK
    block_e = jnp.minimum(jnp.searchsorted(pad_end, bstart, side='right'), N_EXPERTS - 1).astype(jnp.int32)
    block_valid = (bstart < pad_end[-1]).astype(jnp.int32)
    y = _moe_ffn(xin, wg, wu, wd, block_e, block_valid)
    return _combine(x, y, dest, g1.reshape(T), g2.reshape(T))


def _ple_kernel(x_ref, p_ref, g_ref, wgate_ref, wproj_ref, gf_ref, o_ref, *, final):
    x = x_ref[...]
    h = _rms(x, g_ref[...]).astype(BF16)
    gate = jax.nn.sigmoid(jnp.dot(h, wgate_ref[...], preferred_element_type=F32))
    proj = jnp.dot(p_ref[0].astype(BF16), wproj_ref[...], preferred_element_type=F32)
    y = x + gate * proj
    o_ref[...] = _rms(y, gf_ref[...]) if final else y


def _ple(x, p, layer, g, wgate, wproj, g_final, final, tm=512):
    T, D = x.shape
    P = p.shape[2]
    tm = min(tm, T)
    return pl.pallas_call(
        functools.partial(_ple_kernel, final=final),
        grid=(T // tm,),
        in_specs=[pl.BlockSpec((tm, D), lambda i: (i, 0)),
                  pl.BlockSpec((1, tm, P), lambda i: (layer, i, 0)),
                  pl.BlockSpec((1, D), lambda i: (0, 0)),
                  pl.BlockSpec((D, D), lambda i: (0, 0)),
                  pl.BlockSpec((P, D), lambda i: (0, 0)),
                  pl.BlockSpec((1, D), lambda i: (0, 0))],
        out_specs=pl.BlockSpec((tm, D), lambda i: (i, 0)),
        out_shape=jax.ShapeDtypeStruct((T, D), F32),
        compiler_params=_cparams("parallel"),
        name="ple",
    )(x, p, g.reshape(1, D), wgate, wproj, g_final.reshape(1, D))


def kernel(x, p, g_mix, w_in, conv_w, conv_b, conv_ln_g, conv_ln_b, w_conv_out, w_ret_out, w_out, g_ffn, w_dense_gate, w_dense_up, w_dense_down, w_router, w_exp_gate, w_exp_up, w_exp_down, g_ple, w_ple_gate, w_ple_proj, g_final):
    B, S, D = x.shape
    T = B * S
    depth = w_in.shape[0]
    tables, g_chunk = _retention_tables(S)
    bf = lambda w: w.astype(BF16)
    xf = x.reshape(T, D)
    pf = p.reshape(depth, T, p.shape[-1])
    for i in range(depth):
        u = _inproj(xf, g_mix[i], bf(w_in[i]))
        u3 = u.reshape(B, S, u.shape[1])
        c = _conv_branch(u3, conv_w[i], conv_b[i], conv_ln_g[i], conv_ln_b[i])
        og = _retention(u3, tables, g_chunk)
        xf = _merge(xf, c.reshape(T, -1), og.reshape(T, -1), u, bf(w_conv_out[i]), bf(w_ret_out[i]), bf(w_out[i]))
        j = i // 2
        if i % 2 == 0:
            xf = _dense_ffn(xf, g_ffn[i], bf(w_dense_gate[j]), bf(w_dense_up[j]), bf(w_dense_down[j]))
        else:
            xf = _moe_layer(xf, g_ffn[i], w_router[j], bf(w_exp_gate[j]), bf(w_exp_up[j]), bf(w_exp_down[j]))
        xf = _ple(xf, pf, i, g_ple[i], bf(w_ple_gate[i]), bf(w_ple_proj[i]), g_final, i == depth - 1)
    return xf.reshape(B, S, D)
```

```python
import functools

import numpy as np
import jax
import jax.numpy as jnp
from jax import lax
from jax.experimental import pallas as pl
from jax.experimental.pallas import tpu as pltpu

F32 = jnp.float32
BF16 = jnp.bfloat16

EPS = 1e-6
CONV_WIDTH = 31
RET_HEADS = 8
RET_DK = 64
RET_DV = 128
RET_CHUNK = 128
ROPE_BASE = 10000.0
N_EXPERTS = 8

V7X_VMEM_LIMIT_BYTES = 56 * 1024 * 1024
LANES = 128
SUBLANES = 8
CONV_HALO = 32
TOK_TILE = 256
ROW_BLK = 512


def _cparams(*sem):
    return pltpu.CompilerParams(dimension_semantics=sem, vmem_limit_bytes=V7X_VMEM_LIMIT_BYTES)


def _rms(x, g):
    return x * lax.rsqrt(jnp.mean(x * x, axis=-1, keepdims=True) + EPS) * g


def _silu(x):
    return x * jax.nn.sigmoid(x)


def _inproj_kernel(x_ref, g_ref, w_ref, o_ref, h_ref):
    @pl.when(pl.program_id(1) == 0)
    def _():
        h_ref[...] = _rms(x_ref[...], g_ref[...]).astype(BF16)

    o_ref[...] = jnp.dot(h_ref[...], w_ref[...], preferred_element_type=F32).astype(BF16)


def _inproj(x, g, w, tm=1024, tn=3072):
    T, D = x.shape
    N = w.shape[1]
    tm = min(tm, T)
    return pl.pallas_call(
        _inproj_kernel,
        grid=(T // tm, N // tn),
        in_specs=[pl.BlockSpec((tm, D), lambda i, j: (i, 0)),
                  pl.BlockSpec((1, D), lambda i, j: (0, 0)),
                  pl.BlockSpec((D, tn), lambda i, j: (0, j))],
        out_specs=pl.BlockSpec((tm, tn), lambda i, j: (i, j)),
        out_shape=jax.ShapeDtypeStruct((T, N), BF16),
        scratch_shapes=[pltpu.VMEM((tm, D), BF16)],
        compiler_params=_cparams("parallel", "arbitrary"),
        name="inproj",
    )(x, g.reshape(1, D), w)


def _conv_kernel(cur_ref, halo_ref, w_ref, b_ref, lg_ref, lb_ref, o_ref, zbuf, cbuf, *, ts, cc, rc):
    cur = cur_ref[0].astype(F32)
    zbuf[CONV_HALO:CONV_HALO + ts, :] = cur[:, :cc] * jax.nn.sigmoid(cur[:, cc:])
    halo = halo_ref[0].astype(F32)
    zh = halo[:, :cc] * jax.nn.sigmoid(halo[:, cc:])
    zbuf[:CONV_HALO, :] = jnp.where(pl.program_id(1) == 0, 0.0, zh)
    zbuf[CONV_HALO + ts:, :] = jnp.zeros((2 * SUBLANES, cc), F32)
    first = CONV_HALO - (CONV_WIDTH - 1)
    for c in range(cc // LANES):
        cs = slice(c * LANES, (c + 1) * LANES)
        for r in range(ts // rc):
            acc = jnp.broadcast_to(b_ref[:, cs], (rc, LANES))
            for b in range(SUBLANES):
                part = None
                for a in range((first + CONV_WIDTH - 1) // SUBLANES + 1):
                    j = SUBLANES * a + b - first
                    if 0 <= j < CONV_WIDTH:
                        lo = r * rc + SUBLANES * a
                        term = w_ref[j:j + 1, cs] * zbuf[lo:lo + rc + SUBLANES, cs]
                        part = term if part is None else part + term
                acc = acc + part[b:b + rc]
            cbuf[r * rc:(r + 1) * rc, cs] = acc
    z = cbuf[...]
    mu = jnp.mean(z, axis=-1, keepdims=True)
    var = jnp.mean(jnp.square(z - mu), axis=-1, keepdims=True)
    y = (z - mu) * lax.rsqrt(var + EPS) * lg_ref[...] + lb_ref[...]
    o_ref[0] = _silu(y).astype(BF16)


def _conv_branch(u3, conv_w, conv_b, ln_g, ln_b, ts=256, rc=64):
    B, S, _ = u3.shape
    cc = conv_w.shape[1]
    ts = min(ts, S)
    hb = ts // CONV_HALO
    return pl.pallas_call(
        functools.partial(_conv_kernel, ts=ts, cc=cc, rc=rc),
        grid=(B, S // ts),
        in_specs=[pl.BlockSpec((1, ts, 2 * cc), lambda b, i: (b, i, 0)),
                  pl.BlockSpec((1, CONV_HALO, 2 * cc), lambda b, i: (b, jnp.maximum(i * hb - 1, 0), 0)),
                  pl.BlockSpec((CONV_WIDTH, cc), lambda b, i: (0, 0)),
                  pl.BlockSpec((1, cc), lambda b, i: (0, 0)),
                  pl.BlockSpec((1, cc), lambda b, i: (0, 0)),
                  pl.BlockSpec((1, cc), lambda b, i: (0, 0))],
        out_specs=pl.BlockSpec((1, ts, cc), lambda b, i: (b, i, 0)),
        out_shape=jax.ShapeDtypeStruct((B, S, cc), BF16),
        scratch_shapes=[pltpu.VMEM((ts + CONV_HALO + 2 * SUBLANES, cc), F32), pltpu.VMEM((ts, cc), F32)],
        compiler_params=_cparams("parallel", "parallel"),
        name="conv_branch",
    )(u3, u3, conv_w, conv_b.reshape(1, cc), ln_g.reshape(1, cc), ln_b.reshape(1, cc))


def _retention_kernel(q_ref, k_ref, v_ref, gr_ref, cos_ref, sa_ref, sb_ref, dmat_ref, xi_ref, zeta_ref,
                      o_ref, state, *, g_chunk, nch):
    @pl.when(pl.program_id(1) == 0)
    def _():
        state[...] = jnp.zeros_like(state)

    C, pw, vw = RET_CHUNK, 2 * RET_DK, 2 * RET_DV
    hdk = RET_HEADS * RET_DK
    npair = RET_HEADS // 2
    cos, sa, sb = cos_ref[...], sa_ref[...], sb_ref[...]

    def rope(t):
        return t * cos + pltpu.roll(t, hdk - RET_DK // 2, 1) * sa + pltpu.roll(t, RET_DK // 2, 1) * sb

    qf = rope(q_ref[0].astype(F32)) * (RET_DK ** -0.5)
    kf = rope(k_ref[0].astype(F32))
    q = qf.astype(BF16)
    k = kf.astype(BF16)
    lane = lax.broadcasted_iota(jnp.int32, (C, pw), 1)
    keep0 = jnp.where(lane < RET_DK, 1.0, 0.0).astype(BF16)
    keep1 = jnp.where(lane >= RET_DK, 1.0, 0.0).astype(BF16)
    row = lax.broadcasted_iota(jnp.int32, (pw, vw), 0)
    col = lax.broadcasted_iota(jnp.int32, (pw, vw), 1)
    top = row < RET_DK
    diag = top == (col < RET_DV)
    zeros_v = jnp.zeros((C, RET_DV), BF16)
    nt_dims = (((1,), (1,)), ((), ()))
    tn_dims = (((0,), (0,)), ((), ()))

    scores = []
    for ci in range(nch):
        rs = slice(ci * C, (ci + 1) * C)
        for p in range(npair):
            kp = k[rs, p * pw:(p + 1) * pw]
            kcat = jnp.concatenate([kp * keep0, kp * keep1], axis=0)
            s = lax.dot_general(q[rs, p * pw:(p + 1) * pw], kcat, nt_dims, preferred_element_type=F32)
            scores.append((s * dmat_ref[p]).astype(BF16))

    st = [state[p] for p in range(npair)]
    before = []
    for ci in range(nch):
        rs = slice(ci * C, (ci + 1) * C)
        kz = (kf[rs] * zeta_ref[...]).astype(BF16)
        for p in range(npair):
            before.append(st[p].astype(BF16))
            upd = lax.dot_general(kz[:, p * pw:(p + 1) * pw], v_ref[0, rs, p * vw:(p + 1) * vw], tn_dims,
                                  preferred_element_type=F32)
            st[p] = jnp.where(top, g_chunk[2 * p], g_chunk[2 * p + 1]) * st[p] + jnp.where(diag, upd, 0.0)
    for p in range(npair):
        state[p] = st[p]

    for ci in range(nch):
        rs = slice(ci * C, (ci + 1) * C)
        qx = (qf[rs] * xi_ref[...]).astype(BF16)
        for p in range(npair):
            vp = v_ref[0, rs, p * vw:(p + 1) * vw]
            vbd = jnp.concatenate([jnp.concatenate([vp[:, :RET_DV], zeros_v], axis=1),
                                   jnp.concatenate([zeros_v, vp[:, RET_DV:]], axis=1)], axis=0)
            lhs = jnp.concatenate([scores[ci * npair + p], qx[:, p * pw:(p + 1) * pw]], axis=1)
            rhs = jnp.concatenate([vbd, before[ci * npair + p]], axis=0)
            o2 = jnp.dot(lhs, rhs, preferred_element_type=F32)
            for e in range(2):
                vs = slice((2 * p + e) * RET_DV, (2 * p + e + 1) * RET_DV)
                o = o2[:, e * RET_DV:(e + 1) * RET_DV]
                mu = jnp.mean(o, axis=-1, keepdims=True)
                var = jnp.mean(jnp.square(o - mu), axis=-1, keepdims=True)
                o = (o - mu) * lax.rsqrt(var + EPS)
                o_ref[0, rs, vs] = (_silu(gr_ref[0, rs, vs].astype(F32)) * o).astype(BF16)


def _retention_tables(S):
    C, H = RET_CHUNK, RET_HEADS
    inv_freq = ROPE_BASE ** (-jnp.arange(0, RET_DK, 2, dtype=F32) / RET_DK)
    ang = jnp.arange(S, dtype=F32)[:, None] * inv_freq[None, :]
    cos, sin = jnp.cos(ang), jnp.sin(ang)
    zero = jnp.zeros_like(sin)
    cos_f = jnp.tile(cos, (1, 2 * H))
    sin_a = jnp.tile(jnp.concatenate([-sin, zero], axis=1), (1, H))
    sin_b = jnp.tile(jnp.concatenate([zero, sin], axis=1), (1, H))
    log_g = jnp.log(1.0 - jnp.exp2(-5.0 - jnp.arange(H, dtype=F32)))
    pos = jnp.arange(C, dtype=F32)
    diff = pos[:, None] - pos[None, :]
    dmat = jnp.where(diff[None] >= 0, jnp.exp(jnp.maximum(diff, 0.0)[None] * log_g[:, None, None]), 0.0)
    xi = jnp.exp((pos + 1.0)[None] * log_g[:, None])
    zeta = jnp.exp((C - 1.0 - pos)[None] * log_g[:, None])
    xi_f = jnp.repeat(xi.T, RET_DK, axis=1)
    zeta_f = jnp.repeat(zeta.T, RET_DK, axis=1)
    g_chunk = tuple(float(np.exp(C * np.log(1.0 - 2.0 ** (-5.0 - h)))) for h in range(H))
    dmat2 = dmat.reshape(H // 2, 2, C, C).transpose(0, 2, 1, 3).reshape(H // 2, C, 2 * C)
    return (cos_f, sin_a, sin_b, dmat2, xi_f, zeta_f), g_chunk


def _retention(u3, tables, g_chunk, nch=2):
    B, S, _ = u3.shape
    C = RET_CHUNK
    rows = nch * C
    hdk, hdv = RET_HEADS * RET_DK, RET_HEADS * RET_DV
    const = lambda shape: pl.BlockSpec(shape, lambda b, n: (0,) * len(shape))
    return pl.pallas_call(
        functools.partial(_retention_kernel, g_chunk=g_chunk, nch=nch),
        grid=(B, S // rows),
        in_specs=[pl.BlockSpec((1, rows, hdk), lambda b, n: (b, n, 2)),
                  pl.BlockSpec((1, rows, hdk), lambda b, n: (b, n, 3)),
                  pl.BlockSpec((1, rows, hdv), lambda b, n: (b, n, 2)),
                  pl.BlockSpec((1, rows, hdv), lambda b, n: (b, n, 3)),
                  pl.BlockSpec((rows, hdk), lambda b, n: (n, 0)),
                  pl.BlockSpec((rows, hdk), lambda b, n: (n, 0)),
                  pl.BlockSpec((rows, hdk), lambda b, n: (n, 0)),
                  const((RET_HEADS // 2, C, 2 * C)), const((C, hdk)), const((C, hdk))],
        out_specs=pl.BlockSpec((1, rows, hdv), lambda b, n: (b, n, 0)),
        out_shape=jax.ShapeDtypeStruct((B, S, hdv), BF16),
        scratch_shapes=[pltpu.VMEM((RET_HEADS // 2, 2 * RET_DK, 2 * RET_DV), F32)],
        compiler_params=_cparams("parallel", "arbitrary"),
        name="retention",
    )(u3, u3, u3, u3, *tables)


def _merge_kernel(x_ref, c_ref, og_ref, ga_ref, gb_ref, wc_ref, wr_ref, wo_ref, o_ref):
    yc = jnp.dot(c_ref[...], wc_ref[...], preferred_element_type=F32)
    yr = jnp.dot(og_ref[...], wr_ref[...], preferred_element_type=F32)
    merged = (jax.nn.sigmoid(ga_ref[...].astype(F32)) * yc + jax.nn.sigmoid(gb_ref[...].astype(F32)) * yr)
    o_ref[...] = x_ref[...] + jnp.dot(merged.astype(BF16), wo_ref[...], preferred_element_type=F32)


def _merge(x, c, og, u, wc, wr, wo, tm=512):
    T, D = x.shape
    cc = c.shape[1]
    tm = min(tm, T)
    gate_blk = u.shape[1] // D
    return pl.pallas_call(
        _merge_kernel,
        grid=(T // tm,),
        in_specs=[pl.BlockSpec((tm, D), lambda i: (i, 0)),
                  pl.BlockSpec((tm, cc), lambda i: (i, 0)),
                  pl.BlockSpec((tm, D), lambda i: (i, 0)),
                  pl.BlockSpec((tm, D), lambda i: (i, gate_blk - 2)),
                  pl.BlockSpec((tm, D), lambda i: (i, gate_blk - 1)),
                  pl.BlockSpec((cc, D), lambda i: (0, 0)),
                  pl.BlockSpec((D, D), lambda i: (0, 0)),
                  pl.BlockSpec((D, D), lambda i: (0, 0))],
        out_specs=pl.BlockSpec((tm, D), lambda i: (i, 0)),
        out_shape=jax.ShapeDtypeStruct((T, D), F32),
        compiler_params=_cparams("parallel"),
        name="merge_out",
    )(x, c, og, u, u, wc, wr, wo)


def _resident(shape):
    return pl.BlockSpec(shape, lambda *_: (0,) * len(shape), pipeline_mode=pl.Buffered(1))


class _Ple:
    def __init__(self, p, layer, g, wgate, wproj, g_final, final):
        D = wgate.shape[0]
        self.final = final
        self.operands = (p, g.reshape(1, D), wgate, wproj, g_final.reshape(1, D))
        self.layer, self.pdim, self.d = layer, p.shape[2], D

    def specs(self, rows, row_map):
        return [pl.BlockSpec((1, rows, self.pdim), lambda *a: (self.layer, row_map(*a), 0)),
                _resident((1, self.d)), _resident((self.d, self.d)), _resident((self.pdim, self.d)),
                _resident((1, self.d))]


def _ple_tail(x, p_ref, g_ref, wgate_ref, wproj_ref, gf_ref, final):
    h = _rms(x, g_ref[...]).astype(BF16)
    gate = jax.nn.sigmoid(jnp.dot(h, wgate_ref[...], preferred_element_type=F32))
    proj = jnp.dot(p_ref[0].astype(BF16), wproj_ref[...], preferred_element_type=F32)
    y = x + gate * proj
    return _rms(y, gf_ref[...]) if final else y


def _dense_ffn_kernel(x_ref, g_ref, wg_ref, wu_ref, wd_ref, p_ref, gp_ref, wpg_ref, wpp_ref, gf_ref, o_ref, *, final):
    x = x_ref[...]
    h = _rms(x, g_ref[...]).astype(BF16)
    a = _silu(jnp.dot(h, wg_ref[...], preferred_element_type=F32)) * jnp.dot(h, wu_ref[...], preferred_element_type=F32)
    x = x + jnp.dot(a.astype(BF16), wd_ref[...], preferred_element_type=F32)
    o_ref[...] = _ple_tail(x, p_ref, gp_ref, wpg_ref, wpp_ref, gf_ref, final)


def _dense_ffn(x, g, wg, wu, wd, ple, tm=512):
    T, D = x.shape
    F = wg.shape[1]
    tm = min(tm, T)
    return pl.pallas_call(
        functools.partial(_dense_ffn_kernel, final=ple.final),
        grid=(T // tm,),
        in_specs=[pl.BlockSpec((tm, D), lambda i: (i, 0)),
                  _resident((1, D)), _resident((D, F)), _resident((D, F)), _resident((F, D))]
                 + ple.specs(tm, lambda i: i),
        out_specs=pl.BlockSpec((tm, D), lambda i: (i, 0)),
        out_shape=jax.ShapeDtypeStruct((T, D), F32),
        compiler_params=_cparams("parallel"),
        name="dense_ffn_ple",
    )(x, g.reshape(1, D), wg, wu, wd, *ple.operands)


META_ROWS = 8


def _router_kernel(x_ref, g_ref, wr_ref, meta_ref, cnt_ref, run_ref):
    @pl.when(pl.program_id(0) == 0)
    def _():
        run_ref[...] = jnp.zeros_like(run_ref)

    h = _rms(x_ref[...], g_ref[...])
    logits = jnp.dot(h, wr_ref[...], preferred_element_type=F32, precision=lax.Precision.HIGHEST)
    tt = logits.shape[0]
    lane = lax.broadcasted_iota(jnp.int32, (tt, LANES), 1)
    neg = jnp.float32(-jnp.inf)
    logits = jnp.where(lane < N_EXPERTS, logits, neg)
    m1 = jnp.max(logits, axis=1, keepdims=True)
    i1 = jnp.min(jnp.where(logits == m1, lane, LANES), axis=1, keepdims=True)
    mask1 = lane == i1
    rest = jnp.where(mask1, neg, logits)
    m2 = jnp.max(rest, axis=1, keepdims=True)
    i2 = jnp.min(jnp.where(rest == m2, lane, LANES), axis=1, keepdims=True)
    mask2 = lane == i2
    d = jnp.exp(m2 - m1)
    g1 = 1.0 / (1.0 + d)
    g2 = d / (1.0 + d)
    onehot = jnp.where(mask1 | mask2, 1.0, 0.0)
    row = lax.broadcasted_iota(jnp.int32, (tt, tt), 0)
    col = lax.broadcasted_iota(jnp.int32, (tt, tt), 1)
    lower = jnp.where(col < row, 1.0, 0.0).astype(BF16)
    excl = jnp.dot(lower, onehot.astype(BF16), preferred_element_type=F32) + run_ref[...]
    r1 = jnp.sum(jnp.where(mask1, excl, 0.0), axis=1, keepdims=True).astype(jnp.int32)
    r2 = jnp.sum(jnp.where(mask2, excl, 0.0), axis=1, keepdims=True).astype(jnp.int32)
    run_ref[...] += jnp.sum(onehot, axis=0, keepdims=True)
    cnt_ref[0] = run_ref[...].astype(jnp.int32)
    gb1 = pltpu.bitcast(jnp.broadcast_to(g1, (tt, LANES)), jnp.int32)
    gb2 = pltpu.bitcast(jnp.broadcast_to(g2, (tt, LANES)), jnp.int32)
    rec = jnp.where(lane == 0, i1, jnp.where(lane == 1, i2, jnp.where(lane == 2, r1, jnp.where(
        lane == 3, r2, jnp.where(lane == 4, gb1, jnp.where(lane == 5, gb2, 0))))))
    meta_ref[0] = rec.T[:META_ROWS, :]


def _router(x, g, w_router):
    T, D = x.shape
    nt = T // TOK_TILE
    wr = jnp.zeros((D, LANES), F32).at[:, :N_EXPERTS].set(w_router)
    return pl.pallas_call(
        _router_kernel,
        grid=(nt,),
        in_specs=[pl.BlockSpec((TOK_TILE, D), lambda s: (s, 0)),
                  pl.BlockSpec((1, D), lambda s: (0, 0)),
                  pl.BlockSpec((D, LANES), lambda s: (0, 0))],
        out_specs=[pl.BlockSpec((1, META_ROWS, TOK_TILE), lambda s: (s, 0, 0)),
                   pl.BlockSpec((1, 1, LANES), lambda s: (s, 0, 0))],
        out_shape=[jax.ShapeDtypeStruct((nt, META_ROWS, TOK_TILE), jnp.int32),
                   jax.ShapeDtypeStruct((nt, 1, LANES), jnp.int32)],
        scratch_shapes=[pltpu.VMEM((1, LANES), F32)],
        compiler_params=_cparams("arbitrary"),
        name="router",
    )(x, g.reshape(1, D), wr)


def _dispatch_kernel(pad_ref, dest_ref, x_ref, g_ref, xin_hbm, hbuf, sem, zsem, *, nt):
    s = pl.program_id(0)
    slot = s % 2
    hbuf[slot] = _rms(x_ref[...], g_ref[...])

    def issue(t, c):
        for k in range(2):
            pltpu.make_async_copy(hbuf.at[slot, pl.ds(t, 1)],
                                  xin_hbm.at[pl.ds(dest_ref[0, 0, k * TOK_TILE + t], 1)],
                                  sem.at[slot]).start()
        return c

    lax.fori_loop(0, TOK_TILE, issue, 0, unroll=8)

    def drain(sl):
        for _ in range(2):
            pltpu.make_async_copy(hbuf.at[sl], xin_hbm.at[pl.ds(0, TOK_TILE)], sem.at[sl]).wait()

    @pl.when(s > 0)
    def _():
        drain(1 - slot)

    @pl.when(s == nt - 1)
    def _():
        drain(slot)
        hbuf[0] = jnp.zeros((TOK_TILE, hbuf.shape[2]), F32)

        def zissue(r, c):
            pltpu.make_async_copy(hbuf.at[0, pl.ds(0, 1)], xin_hbm.at[pl.ds(r, 1)], zsem).start()
            return c

        def zwait(r, c):
            pltpu.make_async_copy(hbuf.at[0, pl.ds(0, 1)], xin_hbm.at[pl.ds(0, 1)], zsem).wait()
            return c

        for e in range(N_EXPERTS):
            lo, hi = pad_ref[e], pad_ref[N_EXPERTS + e]
            lax.fori_loop(lo, hi, zissue, 0)
            lax.fori_loop(lo, hi, zwait, 0)

        def tissue(r, c):
            pltpu.make_async_copy(hbuf.at[0], xin_hbm.at[pl.ds(pl.multiple_of(r * TOK_TILE, TOK_TILE), TOK_TILE)],
                                  zsem).start()
            return c

        def twait(r, c):
            pltpu.make_async_copy(hbuf.at[0], xin_hbm.at[pl.ds(0, TOK_TILE)], zsem).wait()
            return c

        tail_lo, tail_hi = pad_ref[2 * N_EXPERTS - 1] // TOK_TILE, xin_hbm.shape[0] // TOK_TILE
        lax.fori_loop(tail_lo, tail_hi, tissue, 0)
        lax.fori_loop(tail_lo, tail_hi, twait, 0)


def _dispatch(x, g, dest, pad_info, n_rows):
    T, D = x.shape
    nt = T // TOK_TILE
    return pl.pallas_call(
        functools.partial(_dispatch_kernel, nt=nt),
        grid_spec=pltpu.PrefetchScalarGridSpec(
            num_scalar_prefetch=1,
            grid=(nt,),
            in_specs=[pl.BlockSpec((1, 1, 2 * TOK_TILE), lambda s, pad: (s, 0, 0), memory_space=pltpu.SMEM),
                      pl.BlockSpec((TOK_TILE, D), lambda s, pad: (s, 0)),
                      pl.BlockSpec((1, D), lambda s, pad: (0, 0))],
            out_specs=pl.BlockSpec(memory_space=pl.ANY),
            scratch_shapes=[pltpu.VMEM((2, TOK_TILE, D), F32),
                            pltpu.SemaphoreType.DMA((2,)), pltpu.SemaphoreType.DMA(())]),
        out_shape=jax.ShapeDtypeStruct((n_rows, D), F32),
        compiler_params=_cparams("arbitrary"),
        name="moe_dispatch",
    )(pad_info, dest, x, g.reshape(1, D))


def _cast_kernel(w_ref, o_ref):
    o_ref[0] = w_ref[0, 0].astype(BF16)


def _expert_weights_bf16(w, layer):
    _, E, K, N = w.shape
    bk = K // 2
    return pl.pallas_call(
        _cast_kernel,
        grid=(E, K // bk),
        in_specs=[pl.BlockSpec((1, 1, bk, N), lambda e, i: (layer, e, i, 0))],
        out_specs=pl.BlockSpec((1, bk, N), lambda e, i: (e, i, 0)),
        out_shape=jax.ShapeDtypeStruct((E, K, N), BF16),
        compiler_params=_cparams("parallel", "parallel"),
        name="expert_weight_cast",
    )(w)


def _moe_ffn_kernel(be_ref, bv_ref, x_ref, wg_ref, wu_ref, wd_ref, o_ref, xb_ref):
    b, f = pl.program_id(0), pl.program_id(1)

    @pl.when(f == 0)
    def _():
        xb_ref[...] = x_ref[...].astype(BF16)
        o_ref[...] = jnp.zeros_like(o_ref)

    @pl.when(bv_ref[b] != 0)
    def _():
        x = xb_ref[...]
        a = _silu(jnp.dot(x, wg_ref[0], preferred_element_type=F32)) * jnp.dot(x, wu_ref[0], preferred_element_type=F32)
        o_ref[...] += jnp.dot(a.astype(BF16), wd_ref[0], preferred_element_type=F32)


def _moe_ffn(xin, wg, wu, wd, block_e, block_valid, tf=1792):
    R, D = xin.shape
    F = wg.shape[2]
    nb, nf = R // ROW_BLK, F // tf
    fsel = lambda b, f, bv: jnp.where(bv[b] != 0, f, nf - 1)
    return pl.pallas_call(
        _moe_ffn_kernel,
        grid_spec=pltpu.PrefetchScalarGridSpec(
            num_scalar_prefetch=2,
            grid=(nb, nf),
            in_specs=[pl.BlockSpec((ROW_BLK, D), lambda b, f, be, bv: (jnp.where(bv[b] != 0, b, 0), 0)),
                      pl.BlockSpec((1, D, tf), lambda b, f, be, bv: (be[b], 0, fsel(b, f, bv))),
                      pl.BlockSpec((1, D, tf), lambda b, f, be, bv: (be[b], 0, fsel(b, f, bv))),
                      pl.BlockSpec((1, tf, D), lambda b, f, be, bv: (be[b], fsel(b, f, bv), 0))],
            out_specs=pl.BlockSpec((ROW_BLK, D), lambda b, f, be, bv: (b, 0)),
            scratch_shapes=[pltpu.VMEM((ROW_BLK, D), BF16)]),
        out_shape=jax.ShapeDtypeStruct((R, D), F32),
        compiler_params=_cparams("arbitrary", "arbitrary"),
        name="moe_ffn",
    )(block_e, block_valid, xin, wg, wu, wd)


def _combine_kernel(dcur_ref, dnext_ref, x_ref, g1_ref, g2_ref, y_hbm, p_ref, gp_ref, wpg_ref, wpp_ref, gf_ref,
                    o_ref, ya, yb, sem, *, nt, final):
    s = pl.program_id(0)
    slot = s % 2

    def gather(dref, sl):
        def issue(t, c):
            pltpu.make_async_copy(y_hbm.at[pl.ds(dref[0, 0, t], 1)], ya.at[sl, pl.ds(t, 1)], sem.at[sl]).start()
            pltpu.make_async_copy(y_hbm.at[pl.ds(dref[0, 0, TOK_TILE + t], 1)], yb.at[sl, pl.ds(t, 1)],
                                  sem.at[sl]).start(priority=1)
            return c
        lax.fori_loop(0, TOK_TILE, issue, 0, unroll=8)

    @pl.when(s == 0)
    def _():
        gather(dcur_ref, slot)

    @pl.when(s + 1 < nt)
    def _():
        gather(dnext_ref, 1 - slot)

    for buf in (ya, yb):
        pltpu.make_async_copy(y_hbm.at[pl.ds(0, TOK_TILE)], buf.at[slot], sem.at[slot]).wait()
    x = x_ref[...] + g1_ref[...] * ya[slot] + g2_ref[...] * yb[slot]
    o_ref[...] = _ple_tail(x, p_ref, gp_ref, wpg_ref, wpp_ref, gf_ref, final)


def _combine(x, y, dest, g1, g2, ple):
    T, D = x.shape
    nt = T // TOK_TILE
    smem_spec = lambda imap: pl.BlockSpec((1, 1, 2 * TOK_TILE), imap, memory_space=pltpu.SMEM)
    col_spec = pl.BlockSpec((TOK_TILE, 1), lambda s: (s, 0))
    return pl.pallas_call(
        functools.partial(_combine_kernel, nt=nt, final=ple.final),
        grid=(nt,),
        in_specs=[smem_spec(lambda s: (s, 0, 0)),
                  smem_spec(lambda s: (jnp.minimum(s + 1, nt - 1), 0, 0)),
                  pl.BlockSpec((TOK_TILE, D), lambda s: (s, 0)),
                  col_spec, col_spec,
                  pl.BlockSpec(memory_space=pl.ANY)] + ple.specs(TOK_TILE, lambda s: s),
        out_specs=pl.BlockSpec((TOK_TILE, D), lambda s: (s, 0)),
        out_shape=jax.ShapeDtypeStruct((T, D), F32),
        scratch_shapes=[pltpu.VMEM((2, TOK_TILE, D), F32), pltpu.VMEM((2, TOK_TILE, D), F32),
                        pltpu.SemaphoreType.DMA((2,))],
        compiler_params=_cparams("arbitrary"),
        name="moe_combine_ple",
    )(dest, dest, x, g1.reshape(T, 1), g2.reshape(T, 1), y, *ple.operands)


def _moe_layer(x, g, w_router, wg, wu, wd, ple):
    T, D = x.shape
    nt = T // TOK_TILE
    n_blocks = (2 * T + N_EXPERTS * (ROW_BLK - 1) + ROW_BLK - 1) // ROW_BLK
    meta, cnt = _router(x, g, w_router)
    e1, e2, r1, r2 = meta[:, 0], meta[:, 1], meta[:, 2], meta[:, 3]
    g1 = lax.bitcast_convert_type(meta[:, 4], F32)
    g2 = lax.bitcast_convert_type(meta[:, 5], F32)
    counts = cnt[-1, 0, :N_EXPERTS]
    padded = ((counts + ROW_BLK - 1) // ROW_BLK) * ROW_BLK
    pad_end = jnp.cumsum(padded)
    pad_start = pad_end - padded
    dest = jnp.concatenate([pad_start[e1] + r1, pad_start[e2] + r2], axis=1).reshape(nt, 1, 2 * TOK_TILE)
    pad_info = jnp.concatenate([pad_start + counts, pad_end]).astype(jnp.int32)
    xin = _dispatch(x, g, dest, pad_info, n_blocks * ROW_BLK)
    bstart = jnp.arange(n_blocks, dtype=jnp.int32) * ROW_BLK
    block_e = jnp.minimum(jnp.searchsorted(pad_end, bstart, side='right'), N_EXPERTS - 1).astype(jnp.int32)
    block_valid = (bstart < pad_end[-1]).astype(jnp.int32)
    y = _moe_ffn(xin, wg, wu, wd, block_e, block_valid)
    return _combine(x, y, dest, g1.reshape(T), g2.reshape(T), ple)


def kernel(x, p, g_mix, w_in, conv_w, conv_b, conv_ln_g, conv_ln_b, w_conv_out, w_ret_out, w_out, g_ffn, w_dense_gate, w_dense_up, w_dense_down, w_router, w_exp_gate, w_exp_up, w_exp_down, g_ple, w_ple_gate, w_ple_proj, g_final):
    B, S, D = x.shape
    T = B * S
    depth = w_in.shape[0]
    tables, g_chunk = _retention_tables(S)
    bf = lambda w: w.astype(BF16)
    xf = x.reshape(T, D)
    pf = p.reshape(depth, T, p.shape[-1])
    for i in range(depth):
        u = _inproj(xf, g_mix[i], bf(w_in[i]))
        u3 = u.reshape(B, S, u.shape[1])
        c = _conv_branch(u3, conv_w[i], conv_b[i], conv_ln_g[i], conv_ln_b[i])
        og = _retention(u3, tables, g_chunk)
        xf = _merge(xf, c.reshape(T, -1), og.reshape(T, -1), u, bf(w_conv_out[i]), bf(w_ret_out[i]), bf(w_out[i]))
        j = i // 2
        ple = _Ple(pf, i, g_ple[i], bf(w_ple_gate[i]), bf(w_ple_proj[i]), g_final, i == depth - 1)
        if i % 2 == 0:
            xf = _dense_ffn(xf, g_ffn[i], bf(w_dense_gate[j]), bf(w_dense_up[j]), bf(w_dense_down[j]), ple)
        else:
            xf = _moe_layer(xf, g_ffn[i], w_router[j], _expert_weights_bf16(w_exp_gate, j),
                            _expert_weights_bf16(w_exp_up, j), _expert_weights_bf16(w_exp_down, j), ple)
    return xf.reshape(B, S, D)
```

```python
import functools

import numpy as np
import jax
import jax.numpy as jnp
from jax import lax
from jax.experimental import pallas as pl
from jax.experimental.pallas import tpu as pltpu

F32 = jnp.float32
BF16 = jnp.bfloat16

EPS = 1e-6
CONV_WIDTH = 31
RET_HEADS = 8
RET_DK = 64
RET_DV = 128
RET_CHUNK = 128
ROPE_BASE = 10000.0
N_EXPERTS = 8

V7X_VMEM_LIMIT_BYTES = 56 * 1024 * 1024
LANES = 128
SUBLANES = 8
CONV_HALO = 32
TOK_TILE = 256
ROW_BLK = 512


def _cparams(*sem):
    return pltpu.CompilerParams(dimension_semantics=sem, vmem_limit_bytes=V7X_VMEM_LIMIT_BYTES)


def _rms(x, g):
    return x * lax.rsqrt(jnp.mean(x * x, axis=-1, keepdims=True) + EPS) * g


def _silu(x):
    return x * jax.nn.sigmoid(x)


def _inproj_kernel(x_ref, g_ref, w_ref, o_ref, h_ref):
    @pl.when(pl.program_id(1) == 0)
    def _():
        h_ref[...] = _rms(x_ref[...], g_ref[...]).astype(BF16)

    o_ref[...] = jnp.dot(h_ref[...], w_ref[...], preferred_element_type=F32).astype(BF16)


def _inproj(x, g, w, tm=1024, tn=3072):
    T, D = x.shape
    N = w.shape[1]
    tm = min(tm, T)
    return pl.pallas_call(
        _inproj_kernel,
        grid=(T // tm, N // tn),
        in_specs=[pl.BlockSpec((tm, D), lambda i, j: (i, 0)),
                  pl.BlockSpec((1, D), lambda i, j: (0, 0)),
                  pl.BlockSpec((D, tn), lambda i, j: (0, j))],
        out_specs=pl.BlockSpec((tm, tn), lambda i, j: (i, j)),
        out_shape=jax.ShapeDtypeStruct((T, N), BF16),
        scratch_shapes=[pltpu.VMEM((tm, D), BF16)],
        compiler_params=_cparams("parallel", "arbitrary"),
        name="inproj",
    )(x, g.reshape(1, D), w)


def _conv_kernel(cur_ref, halo_ref, w_ref, b_ref, lg_ref, lb_ref, o_ref, zbuf, cbuf, *, ts, cc, rc):
    cur = cur_ref[0].astype(F32)
    zbuf[CONV_HALO:CONV_HALO + ts, :] = cur[:, :cc] * jax.nn.sigmoid(cur[:, cc:])
    halo = halo_ref[0].astype(F32)
    zh = halo[:, :cc] * jax.nn.sigmoid(halo[:, cc:])
    zbuf[:CONV_HALO, :] = jnp.where(pl.program_id(1) == 0, 0.0, zh)
    zbuf[CONV_HALO + ts:, :] = jnp.zeros((2 * SUBLANES, cc), F32)
    first = CONV_HALO - (CONV_WIDTH - 1)
    for c in range(cc // LANES):
        cs = slice(c * LANES, (c + 1) * LANES)
        for r in range(ts // rc):
            acc = jnp.broadcast_to(b_ref[:, cs], (rc, LANES))
            for b in range(SUBLANES):
                part = None
                for a in range((first + CONV_WIDTH - 1) // SUBLANES + 1):
                    j = SUBLANES * a + b - first
                    if 0 <= j < CONV_WIDTH:
                        lo = r * rc + SUBLANES * a
                        term = w_ref[j:j + 1, cs] * zbuf[lo:lo + rc + SUBLANES, cs]
                        part = term if part is None else part + term
                acc = acc + part[b:b + rc]
            cbuf[r * rc:(r + 1) * rc, cs] = acc
    z = cbuf[...]
    mu = jnp.mean(z, axis=-1, keepdims=True)
    var = jnp.mean(jnp.square(z - mu), axis=-1, keepdims=True)
    y = (z - mu) * lax.rsqrt(var + EPS) * lg_ref[...] + lb_ref[...]
    o_ref[0] = _silu(y).astype(BF16)


def _conv_branch(u3, conv_w, conv_b, ln_g, ln_b, ts=256, rc=64):
    B, S, _ = u3.shape
    cc = conv_w.shape[1]
    ts = min(ts, S)
    hb = ts // CONV_HALO
    return pl.pallas_call(
        functools.partial(_conv_kernel, ts=ts, cc=cc, rc=rc),
        grid=(B, S // ts),
        in_specs=[pl.BlockSpec((1, ts, 2 * cc), lambda b, i: (b, i, 0)),
                  pl.BlockSpec((1, CONV_HALO, 2 * cc), lambda b, i: (b, jnp.maximum(i * hb - 1, 0), 0)),
                  pl.BlockSpec((CONV_WIDTH, cc), lambda b, i: (0, 0)),
                  pl.BlockSpec((1, cc), lambda b, i: (0, 0)),
                  pl.BlockSpec((1, cc), lambda b, i: (0, 0)),
                  pl.BlockSpec((1, cc), lambda b, i: (0, 0))],
        out_specs=pl.BlockSpec((1, ts, cc), lambda b, i: (b, i, 0)),
        out_shape=jax.ShapeDtypeStruct((B, S, cc), BF16),
        scratch_shapes=[pltpu.VMEM((ts + CONV_HALO + 2 * SUBLANES, cc), F32), pltpu.VMEM((ts, cc), F32)],
        compiler_params=_cparams("parallel", "parallel"),
        name="conv_branch",
    )(u3, u3, conv_w, conv_b.reshape(1, cc), ln_g.reshape(1, cc), ln_b.reshape(1, cc))


def _retention_kernel(q_ref, k_ref, v_ref, gr_ref, cos_ref, sa_ref, sb_ref, dmat_ref, xi_ref, zeta_ref,
                      o_ref, state, *, g_chunk, nch):
    @pl.when(pl.program_id(1) == 0)
    def _():
        state[...] = jnp.zeros_like(state)

    C, pw, vw = RET_CHUNK, 2 * RET_DK, 2 * RET_DV
    hdk = RET_HEADS * RET_DK
    npair = RET_HEADS // 2
    cos, sa, sb = cos_ref[...], sa_ref[...], sb_ref[...]

    def rope(t):
        return t * cos + pltpu.roll(t, hdk - RET_DK // 2, 1) * sa + pltpu.roll(t, RET_DK // 2, 1) * sb

    qf = rope(q_ref[0].astype(F32)) * (RET_DK ** -0.5)
    kf = rope(k_ref[0].astype(F32))
    q = qf.astype(BF16)
    k = kf.astype(BF16)
    lane = lax.broadcasted_iota(jnp.int32, (C, pw), 1)
    keep0 = jnp.where(lane < RET_DK, 1.0, 0.0).astype(BF16)
    keep1 = jnp.where(lane >= RET_DK, 1.0, 0.0).astype(BF16)
    row = lax.broadcasted_iota(jnp.int32, (pw, vw), 0)
    col = lax.broadcasted_iota(jnp.int32, (pw, vw), 1)
    top = row < RET_DK
    diag = top == (col < RET_DV)
    zeros_v = jnp.zeros((C, RET_DV), BF16)
    nt_dims = (((1,), (1,)), ((), ()))
    tn_dims = (((0,), (0,)), ((), ()))

    scores = []
    for ci in range(nch):
        rs = slice(ci * C, (ci + 1) * C)
        for p in range(npair):
            kp = k[rs, p * pw:(p + 1) * pw]
            kcat = jnp.concatenate([kp * keep0, kp * keep1], axis=0)
            s = lax.dot_general(q[rs, p * pw:(p + 1) * pw], kcat, nt_dims, preferred_element_type=F32)
            scores.append((s * dmat_ref[p]).astype(BF16))

    st = [state[p] for p in range(npair)]
    before = []
    for ci in range(nch):
        rs = slice(ci * C, (ci + 1) * C)
        kz = (kf[rs] * zeta_ref[...]).astype(BF16)
        for p in range(npair):
            before.append(st[p].astype(BF16))
            upd = lax.dot_general(kz[:, p * pw:(p + 1) * pw], v_ref[0, rs, p * vw:(p + 1) * vw], tn_dims,
                                  preferred_element_type=F32)
            st[p] = jnp.where(top, g_chunk[2 * p], g_chunk[2 * p + 1]) * st[p] + jnp.where(diag, upd, 0.0)
    for p in range(npair):
        state[p] = st[p]

    for ci in range(nch):
        rs = slice(ci * C, (ci + 1) * C)
        qx = (qf[rs] * xi_ref[...]).astype(BF16)
        for p in range(npair):
            vp = v_ref[0, rs, p * vw:(p + 1) * vw]
            vbd = jnp.concatenate([jnp.concatenate([vp[:, :RET_DV], zeros_v], axis=1),
                                   jnp.concatenate([zeros_v, vp[:, RET_DV:]], axis=1)], axis=0)
            lhs = jnp.concatenate([scores[ci * npair + p], qx[:, p * pw:(p + 1) * pw]], axis=1)
            rhs = jnp.concatenate([vbd, before[ci * npair + p]], axis=0)
            o2 = jnp.dot(lhs, rhs, preferred_element_type=F32)
            for e in range(2):
                vs = slice((2 * p + e) * RET_DV, (2 * p + e + 1) * RET_DV)
                o = o2[:, e * RET_DV:(e + 1) * RET_DV]
                mu = jnp.mean(o, axis=-1, keepdims=True)
                var = jnp.mean(jnp.square(o - mu), axis=-1, keepdims=True)
                o = (o - mu) * lax.rsqrt(var + EPS)
                o_ref[0, rs, vs] = (_silu(gr_ref[0, rs, vs].astype(F32)) * o).astype(BF16)


def _retention_tables(S):
    C, H = RET_CHUNK, RET_HEADS
    inv_freq = ROPE_BASE ** (-jnp.arange(0, RET_DK, 2, dtype=F32) / RET_DK)
    ang = jnp.arange(S, dtype=F32)[:, None] * inv_freq[None, :]
    cos, sin = jnp.cos(ang), jnp.sin(ang)
    zero = jnp.zeros_like(sin)
    cos_f = jnp.tile(cos, (1, 2 * H))
    sin_a = jnp.tile(jnp.concatenate([-sin, zero], axis=1), (1, H))
    sin_b = jnp.tile(jnp.concatenate([zero, sin], axis=1), (1, H))
    log_g = jnp.log(1.0 - jnp.exp2(-5.0 - jnp.arange(H, dtype=F32)))
    pos = jnp.arange(C, dtype=F32)
    diff = pos[:, None] - pos[None, :]
    dmat = jnp.where(diff[None] >= 0, jnp.exp(jnp.maximum(diff, 0.0)[None] * log_g[:, None, None]), 0.0)
    xi = jnp.exp((pos + 1.0)[None] * log_g[:, None])
    zeta = jnp.exp((C - 1.0 - pos)[None] * log_g[:, None])
    xi_f = jnp.repeat(xi.T, RET_DK, axis=1)
    zeta_f = jnp.repeat(zeta.T, RET_DK, axis=1)
    g_chunk = tuple(float(np.exp(C * np.log(1.0 - 2.0 ** (-5.0 - h)))) for h in range(H))
    dmat2 = dmat.reshape(H // 2, 2, C, C).transpose(0, 2, 1, 3).reshape(H // 2, C, 2 * C)
    return (cos_f, sin_a, sin_b, dmat2, xi_f, zeta_f), g_chunk


def _retention(u3, tables, g_chunk, nch=2):
    B, S, _ = u3.shape
    C = RET_CHUNK
    rows = nch * C
    hdk, hdv = RET_HEADS * RET_DK, RET_HEADS * RET_DV
    const = lambda shape: pl.BlockSpec(shape, lambda b, n: (0,) * len(shape))
    return pl.pallas_call(
        functools.partial(_retention_kernel, g_chunk=g_chunk, nch=nch),
        grid=(B, S // rows),
        in_specs=[pl.BlockSpec((1, rows, hdk), lambda b, n: (b, n, 2)),
                  pl.BlockSpec((1, rows, hdk), lambda b, n: (b, n, 3)),
                  pl.BlockSpec((1, rows, hdv), lambda b, n: (b, n, 2)),
                  pl.BlockSpec((1, rows, hdv), lambda b, n: (b, n, 3)),
                  pl.BlockSpec((rows, hdk), lambda b, n: (n, 0)),
                  pl.BlockSpec((rows, hdk), lambda b, n: (n, 0)),
                  pl.BlockSpec((rows, hdk), lambda b, n: (n, 0)),
                  const((RET_HEADS // 2, C, 2 * C)), const((C, hdk)), const((C, hdk))],
        out_specs=pl.BlockSpec((1, rows, hdv), lambda b, n: (b, n, 0)),
        out_shape=jax.ShapeDtypeStruct((B, S, hdv), BF16),
        scratch_shapes=[pltpu.VMEM((RET_HEADS // 2, 2 * RET_DK, 2 * RET_DV), F32)],
        compiler_params=_cparams("parallel", "arbitrary"),
        name="retention",
    )(u3, u3, u3, u3, *tables)


def _merge_kernel(x_ref, c_ref, og_ref, ga_ref, gb_ref, wc_ref, wr_ref, wo_ref, o_ref):
    yc = jnp.dot(c_ref[...], wc_ref[...], preferred_element_type=F32)
    yr = jnp.dot(og_ref[...], wr_ref[...], preferred_element_type=F32)
    merged = (jax.nn.sigmoid(ga_ref[...].astype(F32)) * yc + jax.nn.sigmoid(gb_ref[...].astype(F32)) * yr)
    o_ref[...] = x_ref[...] + jnp.dot(merged.astype(BF16), wo_ref[...], preferred_element_type=F32)


def _merge(x, c, og, u, wc, wr, wo, tm=512):
    T, D = x.shape
    cc = c.shape[1]
    tm = min(tm, T)
    gate_blk = u.shape[1] // D
    return pl.pallas_call(
        _merge_kernel,
        grid=(T // tm,),
        in_specs=[pl.BlockSpec((tm, D), lambda i: (i, 0)),
                  pl.BlockSpec((tm, cc), lambda i: (i, 0)),
                  pl.BlockSpec((tm, D), lambda i: (i, 0)),
                  pl.BlockSpec((tm, D), lambda i: (i, gate_blk - 2)),
                  pl.BlockSpec((tm, D), lambda i: (i, gate_blk - 1)),
                  pl.BlockSpec((cc, D), lambda i: (0, 0)),
                  pl.BlockSpec((D, D), lambda i: (0, 0)),
                  pl.BlockSpec((D, D), lambda i: (0, 0))],
        out_specs=pl.BlockSpec((tm, D), lambda i: (i, 0)),
        out_shape=jax.ShapeDtypeStruct((T, D), F32),
        compiler_params=_cparams("parallel"),
        name="merge_out",
    )(x, c, og, u, u, wc, wr, wo)


def _resident(shape):
    return pl.BlockSpec(shape, lambda *_: (0,) * len(shape), pipeline_mode=pl.Buffered(1))


class _Ple:
    def __init__(self, p, layer, g, wgate, wproj, g_final, final):
        D = wgate.shape[0]
        self.final = final
        self.operands = (p, g.reshape(1, D), wgate, wproj, g_final.reshape(1, D))
        self.layer, self.pdim, self.d = layer, p.shape[2], D

    def specs(self, rows, row_map):
        return [pl.BlockSpec((1, rows, self.pdim), lambda *a: (self.layer, row_map(*a), 0)),
                _resident((1, self.d)), _resident((self.d, self.d)), _resident((self.pdim, self.d)),
                _resident((1, self.d))]


def _ple_tail(x, p_ref, g_ref, wgate_ref, wproj_ref, gf_ref, final):
    h = _rms(x, g_ref[...]).astype(BF16)
    gate = jax.nn.sigmoid(jnp.dot(h, wgate_ref[...], preferred_element_type=F32))
    proj = jnp.dot(p_ref[0].astype(BF16), wproj_ref[...], preferred_element_type=F32)
    y = x + gate * proj
    return _rms(y, gf_ref[...]) if final else y


def _dense_ffn_kernel(x_ref, g_ref, wg_ref, wu_ref, wd_ref, p_ref, gp_ref, wpg_ref, wpp_ref, gf_ref, o_ref, *, final):
    x = x_ref[...]
    h = _rms(x, g_ref[...]).astype(BF16)
    a = _silu(jnp.dot(h, wg_ref[...], preferred_element_type=F32)) * jnp.dot(h, wu_ref[...], preferred_element_type=F32)
    x = x + jnp.dot(a.astype(BF16), wd_ref[...], preferred_element_type=F32)
    o_ref[...] = _ple_tail(x, p_ref, gp_ref, wpg_ref, wpp_ref, gf_ref, final)


def _dense_ffn(x, g, wg, wu, wd, ple, tm=512):
    T, D = x.shape
    F = wg.shape[1]
    tm = min(tm, T)
    return pl.pallas_call(
        functools.partial(_dense_ffn_kernel, final=ple.final),
        grid=(T // tm,),
        in_specs=[pl.BlockSpec((tm, D), lambda i: (i, 0)),
                  _resident((1, D)), _resident((D, F)), _resident((D, F)), _resident((F, D))]
                 + ple.specs(tm, lambda i: i),
        out_specs=pl.BlockSpec((tm, D), lambda i: (i, 0)),
        out_shape=jax.ShapeDtypeStruct((T, D), F32),
        compiler_params=_cparams("parallel"),
        name="dense_ffn_ple",
    )(x, g.reshape(1, D), wg, wu, wd, *ple.operands)


META_ROWS = 8


def _router_kernel(x_ref, g_ref, wr_ref, meta_ref, cnt_ref, run_ref):
    @pl.when(pl.program_id(0) == 0)
    def _():
        run_ref[...] = jnp.zeros_like(run_ref)

    h = _rms(x_ref[...], g_ref[...])
    h_hi = h.astype(BF16)
    h_lo = (h - h_hi.astype(F32)).astype(BF16)
    w_hi, w_lo = wr_ref[0], wr_ref[1]
    logits = (jnp.dot(h_hi, w_hi, preferred_element_type=F32) + jnp.dot(h_lo, w_hi, preferred_element_type=F32)
              + jnp.dot(h_hi, w_lo, preferred_element_type=F32))
    tt = logits.shape[0]
    lane = lax.broadcasted_iota(jnp.int32, (tt, LANES), 1)
    neg = jnp.float32(-jnp.inf)
    logits = jnp.where(lane < N_EXPERTS, logits, neg)
    m1 = jnp.max(logits, axis=1, keepdims=True)
    i1 = jnp.min(jnp.where(logits == m1, lane, LANES), axis=1, keepdims=True)
    mask1 = lane == i1
    rest = jnp.where(mask1, neg, logits)
    m2 = jnp.max(rest, axis=1, keepdims=True)
    i2 = jnp.min(jnp.where(rest == m2, lane, LANES), axis=1, keepdims=True)
    mask2 = lane == i2
    d = jnp.exp(m2 - m1)
    g1 = 1.0 / (1.0 + d)
    g2 = d / (1.0 + d)
    onehot = jnp.where(mask1 | mask2, 1.0, 0.0)
    row = lax.broadcasted_iota(jnp.int32, (tt, tt), 0)
    col = lax.broadcasted_iota(jnp.int32, (tt, tt), 1)
    lower = jnp.where(col < row, 1.0, 0.0).astype(BF16)
    excl = jnp.dot(lower, onehot.astype(BF16), preferred_element_type=F32) + run_ref[...]
    r1 = jnp.sum(jnp.where(mask1, excl, 0.0), axis=1, keepdims=True).astype(jnp.int32)
    r2 = jnp.sum(jnp.where(mask2, excl, 0.0), axis=1, keepdims=True).astype(jnp.int32)
    run_ref[...] += jnp.sum(onehot, axis=0, keepdims=True)
    cnt_ref[0] = run_ref[...].astype(jnp.int32)
    gb1 = pltpu.bitcast(jnp.broadcast_to(g1, (tt, LANES)), jnp.int32)
    gb2 = pltpu.bitcast(jnp.broadcast_to(g2, (tt, LANES)), jnp.int32)
    rec = jnp.where(lane == 0, i1, jnp.where(lane == 1, i2, jnp.where(lane == 2, r1, jnp.where(
        lane == 3, r2, jnp.where(lane == 4, gb1, jnp.where(lane == 5, gb2, 0))))))
    meta_ref[0] = rec.T[:META_ROWS, :]


def _router(x, g, w_router):
    T, D = x.shape
    nt = T // TOK_TILE
    wr = jnp.zeros((D, LANES), F32).at[:, :N_EXPERTS].set(w_router)
    wr_hi = wr.astype(BF16)
    wr = jnp.stack([wr_hi, (wr - wr_hi.astype(F32)).astype(BF16)])
    return pl.pallas_call(
        _router_kernel,
        grid=(nt,),
        in_specs=[pl.BlockSpec((TOK_TILE, D), lambda s: (s, 0)),
                  pl.BlockSpec((1, D), lambda s: (0, 0)),
                  pl.BlockSpec((2, D, LANES), lambda s: (0, 0, 0))],
        out_specs=[pl.BlockSpec((1, META_ROWS, TOK_TILE), lambda s: (s, 0, 0)),
                   pl.BlockSpec((1, 1, LANES), lambda s: (s, 0, 0))],
        out_shape=[jax.ShapeDtypeStruct((nt, META_ROWS, TOK_TILE), jnp.int32),
                   jax.ShapeDtypeStruct((nt, 1, LANES), jnp.int32)],
        scratch_shapes=[pltpu.VMEM((1, LANES), F32)],
        compiler_params=_cparams("arbitrary"),
        name="router",
    )(x, g.reshape(1, D), wr)


def _dispatch_kernel(pad_ref, dest_ref, x_ref, g_ref, wg_ref, wu_ref, wd_ref,
                     xin_hbm, wgo_ref, wuo_ref, wdo_ref, hbuf, sem, zsem, *, nt):
    s = pl.program_id(0)
    slot = s % 2
    hbuf[slot] = _rms(x_ref[...], g_ref[...])

    for t in range(TOK_TILE):
        for k in range(2):
            pltpu.make_async_copy(hbuf.at[slot, pl.ds(t, 1)],
                                  xin_hbm.at[pl.ds(dest_ref[0, 0, k * TOK_TILE + t], 1)],
                                  sem.at[slot]).start()

    for w_ref, wo_ref in ((wg_ref, wgo_ref), (wu_ref, wuo_ref), (wd_ref, wdo_ref)):
        wo_ref[...] = w_ref[0].astype(BF16)

    def drain(sl):
        for _ in range(2):
            pltpu.make_async_copy(hbuf.at[sl], xin_hbm.at[pl.ds(0, TOK_TILE)], sem.at[sl]).wait()

    @pl.when(s > 0)
    def _():
        drain(1 - slot)

    @pl.when(s == nt - 1)
    def _():
        drain(slot)
        hbuf[0] = jnp.zeros(hbuf.shape[1:], F32)

        def zissue(r, c):
            pltpu.make_async_copy(hbuf.at[0, pl.ds(0, 1)], xin_hbm.at[pl.ds(r, 1)], zsem).start()
            return c

        def zwait(r, c):
            pltpu.make_async_copy(hbuf.at[0, pl.ds(0, 1)], xin_hbm.at[pl.ds(0, 1)], zsem).wait()
            return c

        for e in range(N_EXPERTS):
            lo, hi = pad_ref[e], pad_ref[N_EXPERTS + e]
            lax.fori_loop(lo, hi, zissue, 0)
            lax.fori_loop(lo, hi, zwait, 0)

        def tissue(r, c):
            pltpu.make_async_copy(hbuf.at[0], xin_hbm.at[pl.ds(pl.multiple_of(r * TOK_TILE, TOK_TILE), TOK_TILE)],
                                  zsem).start()
            return c

        def twait(r, c):
            pltpu.make_async_copy(hbuf.at[0], xin_hbm.at[pl.ds(0, TOK_TILE)], zsem).wait()
            return c

        tail_lo, tail_hi = pad_ref[2 * N_EXPERTS - 1] // TOK_TILE, xin_hbm.shape[0] // TOK_TILE
        lax.fori_loop(tail_lo, tail_hi, tissue, 0)
        lax.fori_loop(tail_lo, tail_hi, twait, 0)


def _dispatch(x, g, dest, pad_info, n_rows, layer, w_gate, w_up, w_down):
    T, D = x.shape
    nt = T // TOK_TILE
    _, E, K, F = w_gate.shape
    flat = lambda w: w.reshape(w.shape[0], E * w.shape[2], w.shape[3])
    ck, cf = E * K // nt, E * F // nt
    w_in_spec = lambda rows, cols: pl.BlockSpec((1, rows, cols), lambda s, pad: (layer, s, 0))
    w_out_spec = lambda rows, cols: pl.BlockSpec((rows, cols), lambda s, pad: (s, 0))
    xin, wg, wu, wd = pl.pallas_call(
        functools.partial(_dispatch_kernel, nt=nt),
        grid_spec=pltpu.PrefetchScalarGridSpec(
            num_scalar_prefetch=1,
            grid=(nt,),
            in_specs=[pl.BlockSpec((1, 1, 2 * TOK_TILE), lambda s, pad: (s, 0, 0), memory_space=pltpu.SMEM),
                      pl.BlockSpec((TOK_TILE, D), lambda s, pad: (s, 0)),
                      pl.BlockSpec((1, D), lambda s, pad: (0, 0)),
                      w_in_spec(ck, F), w_in_spec(ck, F), w_in_spec(cf, K)],
            out_specs=[pl.BlockSpec(memory_space=pl.ANY),
                       w_out_spec(ck, F), w_out_spec(ck, F), w_out_spec(cf, K)],
            scratch_shapes=[pltpu.VMEM((2, TOK_TILE, D), F32),
                            pltpu.SemaphoreType.DMA((2,)), pltpu.SemaphoreType.DMA(())]),
        out_shape=[jax.ShapeDtypeStruct((n_rows, D), F32),
                   jax.ShapeDtypeStruct((E * K, F), BF16), jax.ShapeDtypeStruct((E * K, F), BF16),
                   jax.ShapeDtypeStruct((E * F, K), BF16)],
        compiler_params=_cparams("arbitrary"),
        name="moe_dispatch",
    )(pad_info, dest, x, g.reshape(1, D), flat(w_gate), flat(w_up), flat(w_down))
    return xin, wg.reshape(E, K, F), wu.reshape(E, K, F), wd.reshape(E, F, K)


def _moe_ffn_kernel(be_ref, bv_ref, x_ref, wg_ref, wu_ref, wd_ref, o_ref, xb_ref):
    b, f = pl.program_id(0), pl.program_id(1)

    @pl.when(f == 0)
    def _():
        xb_ref[...] = x_ref[...].astype(BF16)
        o_ref[...] = jnp.zeros_like(o_ref)

    @pl.when(bv_ref[b] != 0)
    def _():
        x = xb_ref[...]
        a = _silu(jnp.dot(x, wg_ref[0], preferred_element_type=F32)) * jnp.dot(x, wu_ref[0], preferred_element_type=F32)
        o_ref[...] += jnp.dot(a.astype(BF16), wd_ref[0], preferred_element_type=F32)


def _moe_ffn(xin, wg, wu, wd, block_e, block_valid, tf=1792):
    R, D = xin.shape
    F = wg.shape[2]
    nb, nf = R // ROW_BLK, F // tf
    fsel = lambda b, f, bv: jnp.where(bv[b] != 0, f, nf - 1)
    return pl.pallas_call(
        _moe_ffn_kernel,
        grid_spec=pltpu.PrefetchScalarGridSpec(
            num_scalar_prefetch=2,
            grid=(nb, nf),
            in_specs=[pl.BlockSpec((ROW_BLK, D), lambda b, f, be, bv: (jnp.where(bv[b] != 0, b, 0), 0)),
                      pl.BlockSpec((1, D, tf), lambda b, f, be, bv: (be[b], 0, fsel(b, f, bv))),
                      pl.BlockSpec((1, D, tf), lambda b, f, be, bv: (be[b], 0, fsel(b, f, bv))),
                      pl.BlockSpec((1, tf, D), lambda b, f, be, bv: (be[b], fsel(b, f, bv), 0))],
            out_specs=pl.BlockSpec((ROW_BLK, D), lambda b, f, be, bv: (b, 0)),
            scratch_shapes=[pltpu.VMEM((ROW_BLK, D), BF16)]),
        out_shape=jax.ShapeDtypeStruct((R, D), F32),
        compiler_params=_cparams("arbitrary", "arbitrary"),
        name="moe_ffn",
    )(block_e, block_valid, xin, wg, wu, wd)


def _combine_kernel(dcur_ref, dnext_ref, x_ref, g1_ref, g2_ref, y_hbm, p_ref, gp_ref, wpg_ref, wpp_ref, gf_ref,
                    o_ref, ya, yb, sem, *, nt, final):
    s = pl.program_id(0)
    slot = s % 2

    def issue(dref, sl, t):
        pltpu.make_async_copy(y_hbm.at[pl.ds(dref[0, 0, t], 1)], ya.at[sl, pl.ds(t, 1)], sem.at[sl]).start()
        pltpu.make_async_copy(y_hbm.at[pl.ds(dref[0, 0, TOK_TILE + t], 1)], yb.at[sl, pl.ds(t, 1)],
                              sem.at[sl]).start(priority=1)

    def drain(sl):
        for buf in (ya, yb):
            pltpu.make_async_copy(y_hbm.at[pl.ds(0, TOK_TILE)], buf.at[sl], sem.at[sl]).wait()

    @pl.when(s == 0)
    def _():
        lax.fori_loop(0, TOK_TILE, lambda t, c: issue(dcur_ref, slot, t) or c, 0, unroll=8)

    drain(slot)
    for t in range(TOK_TILE):
        issue(dnext_ref, 1 - slot, t)
    x = x_ref[...] + g1_ref[...] * ya[slot] + g2_ref[...] * yb[slot]
    o_ref[...] = _ple_tail(x, p_ref, gp_ref, wpg_ref, wpp_ref, gf_ref, final)

    @pl.when(s == nt - 1)
    def _():
        drain(1 - slot)


def _combine(x, y, dest, g1, g2, ple):
    T, D = x.shape
    nt = T // TOK_TILE
    smem_spec = lambda imap: pl.BlockSpec((1, 1, 2 * TOK_TILE), imap, memory_space=pltpu.SMEM)
    col_spec = pl.BlockSpec((TOK_TILE, 1), lambda s: (s, 0))
    return pl.pallas_call(
        functools.partial(_combine_kernel, nt=nt, final=ple.final),
        grid=(nt,),
        in_specs=[smem_spec(lambda s: (s, 0, 0)),
                  smem_spec(lambda s: (jnp.minimum(s + 1, nt - 1), 0, 0)),
                  pl.BlockSpec((TOK_TILE, D), lambda s: (s, 0)),
                  col_spec, col_spec,
                  pl.BlockSpec(memory_space=pl.ANY)] + ple.specs(TOK_TILE, lambda s: s),
        out_specs=pl.BlockSpec((TOK_TILE, D), lambda s: (s, 0)),
        out_shape=jax.ShapeDtypeStruct((T, D), F32),
        scratch_shapes=[pltpu.VMEM((2, TOK_TILE, D), F32), pltpu.VMEM((2, TOK_TILE, D), F32),
                        pltpu.SemaphoreType.DMA((2,))],
        compiler_params=_cparams("arbitrary"),
        name="moe_combine_ple",
    )(dest, dest, x, g1.reshape(T, 1), g2.reshape(T, 1), y, *ple.operands)


def _moe_layer(x, g, w_router, layer, w_gate, w_up, w_down, ple):
    T, D = x.shape
    nt = T // TOK_TILE
    n_blocks = (2 * T + N_EXPERTS * (ROW_BLK - 1) + ROW_BLK - 1) // ROW_BLK
    meta, cnt = _router(x, g, w_router)
    e1, e2, r1, r2 = meta[:, 0], meta[:, 1], meta[:, 2], meta[:, 3]
    g1 = lax.bitcast_convert_type(meta[:, 4], F32)
    g2 = lax.bitcast_convert_type(meta[:, 5], F32)
    counts = cnt[-1, 0, :N_EXPERTS]
    padded = ((counts + ROW_BLK - 1) // ROW_BLK) * ROW_BLK
    pad_end = jnp.cumsum(padded)
    pad_start = pad_end - padded
    dest = jnp.concatenate([pad_start[e1] + r1, pad_start[e2] + r2], axis=1).reshape(nt, 1, 2 * TOK_TILE)
    pad_info = jnp.concatenate([pad_start + counts, pad_end]).astype(jnp.int32)
    xin, wg, wu, wd = _dispatch(x, g, dest, pad_info, n_blocks * ROW_BLK, layer, w_gate, w_up, w_down)
    bstart = jnp.arange(n_blocks, dtype=jnp.int32) * ROW_BLK
    block_e = jnp.minimum(jnp.sum(bstart[:, None] >= pad_end[None, :], axis=1), N_EXPERTS - 1).astype(jnp.int32)
    block_valid = (bstart < pad_end[-1]).astype(jnp.int32)
    y = _moe_ffn(xin, wg, wu, wd, block_e, block_valid)
    return _combine(x, y, dest, g1.reshape(T), g2.reshape(T), ple)


def kernel(x, p, g_mix, w_in, conv_w, conv_b, conv_ln_g, conv_ln_b, w_conv_out, w_ret_out, w_out, g_ffn, w_dense_gate, w_dense_up, w_dense_down, w_router, w_exp_gate, w_exp_up, w_exp_down, g_ple, w_ple_gate, w_ple_proj, g_final):
    B, S, D = x.shape
    T = B * S
    depth = w_in.shape[0]
    tables, g_chunk = _retention_tables(S)
    bf = lambda w: w.astype(BF16)
    xf = x.reshape(T, D)
    pf = p.reshape(depth, T, p.shape[-1])
    for i in range(depth):
        u = _inproj(xf, g_mix[i], bf(w_in[i]))
        u3 = u.reshape(B, S, u.shape[1])
        c = _conv_branch(u3, conv_w[i], conv_b[i], conv_ln_g[i], conv_ln_b[i])
        og = _retention(u3, tables, g_chunk)
        xf = _merge(xf, c.reshape(T, -1), og.reshape(T, -1), u, bf(w_conv_out[i]), bf(w_ret_out[i]), bf(w_out[i]))
        j = i // 2
        ple = _Ple(pf, i, g_ple[i], bf(w_ple_gate[i]), bf(w_ple_proj[i]), g_final, i == depth - 1)
        if i % 2 == 0:
            xf = _dense_ffn(xf, g_ffn[i], bf(w_dense_gate[j]), bf(w_dense_up[j]), bf(w_dense_down[j]), ple)
        else:
            xf = _moe_layer(xf, g_ffn[i], w_router[j], j, w_exp_gate, w_exp_up, w_exp_down, ple)
    return xf.reshape(B, S, D)
```

```python
import functools

import numpy as np
import jax
import jax.numpy as jnp
from jax import lax
from jax.experimental import pallas as pl
from jax.experimental.pallas import tpu as pltpu

F32 = jnp.float32
BF16 = jnp.bfloat16

EPS = 1e-6
CONV_WIDTH = 31
RET_HEADS = 8
RET_DK = 64
RET_DV = 128
RET_CHUNK = 128
ROPE_BASE = 10000.0
N_EXPERTS = 8

V7X_VMEM_LIMIT_BYTES = 56 * 1024 * 1024
LANES = 128
SUBLANES = 8
CONV_HALO = 32
TOK_TILE = 256
ROW_BLK = 512


def _cparams(*sem):
    return pltpu.CompilerParams(dimension_semantics=sem, vmem_limit_bytes=V7X_VMEM_LIMIT_BYTES)


def _rms(x, g):
    return x * lax.rsqrt(jnp.mean(x * x, axis=-1, keepdims=True) + EPS) * g


def _silu(x):
    return x * jax.nn.sigmoid(x)


def _inproj_kernel(x_ref, g_ref, w_ref, o_ref, h_ref):
    @pl.when(pl.program_id(1) == 0)
    def _():
        h_ref[...] = _rms(x_ref[...], g_ref[...]).astype(BF16)

    o_ref[...] = jnp.dot(h_ref[...], w_ref[...], preferred_element_type=F32).astype(BF16)


def _inproj(x, g, w, tm=1024, tn=3072):
    T, D = x.shape
    N = w.shape[1]
    tm = min(tm, T)
    return pl.pallas_call(
        _inproj_kernel,
        grid=(T // tm, N // tn),
        in_specs=[pl.BlockSpec((tm, D), lambda i, j: (i, 0)),
                  pl.BlockSpec((1, D), lambda i, j: (0, 0)),
                  pl.BlockSpec((D, tn), lambda i, j: (0, j))],
        out_specs=pl.BlockSpec((tm, tn), lambda i, j: (i, j)),
        out_shape=jax.ShapeDtypeStruct((T, N), BF16),
        scratch_shapes=[pltpu.VMEM((tm, D), BF16)],
        compiler_params=_cparams("parallel", "arbitrary"),
        name="inproj",
    )(x, g.reshape(1, D), w)


CONV_ROW_CHUNK = 64


def _glu(t):
    half = t.shape[1] // 2
    t = t.astype(F32)
    return t[:, :half] * jax.nn.sigmoid(t[:, half:])


def _conv_swish(cur, halo_z, w_ref, b_ref, lg_ref, lb_ref, zbuf, cbuf):
    ts, cc, rc = cbuf.shape[0], cbuf.shape[1], CONV_ROW_CHUNK
    zbuf[CONV_HALO:CONV_HALO + ts, :] = _glu(cur)
    zbuf[:CONV_HALO, :] = halo_z
    zbuf[CONV_HALO + ts:, :] = jnp.zeros((2 * SUBLANES, cc), F32)
    first = CONV_HALO - (CONV_WIDTH - 1)
    for c in range(cc // LANES):
        cs = slice(c * LANES, (c + 1) * LANES)
        for r in range(ts // rc):
            acc = jnp.broadcast_to(b_ref[:, cs], (rc, LANES))
            for b in range(SUBLANES):
                part = None
                for a in range((first + CONV_WIDTH - 1) // SUBLANES + 1):
                    j = SUBLANES * a + b - first
                    if 0 <= j < CONV_WIDTH:
                        lo = r * rc + SUBLANES * a
                        term = w_ref[j:j + 1, cs] * zbuf[lo:lo + rc + SUBLANES, cs]
                        part = term if part is None else part + term
                acc = acc + part[b:b + rc]
            cbuf[r * rc:(r + 1) * rc, cs] = acc
    z = cbuf[...]
    mu = jnp.mean(z, axis=-1, keepdims=True)
    var = jnp.mean(jnp.square(z - mu), axis=-1, keepdims=True)
    y = (z - mu) * lax.rsqrt(var + EPS) * lg_ref[...] + lb_ref[...]
    return _silu(y).astype(BF16)


def _retention_rows(q_in, k_in, v_of, gr_of, cos, sa, sb, dmat_ref, xi_ref, zeta_ref, store, state, g_chunk):
    C, pw, vw = RET_CHUNK, 2 * RET_DK, 2 * RET_DV
    hdk = RET_HEADS * RET_DK
    npair = RET_HEADS // 2
    nch = q_in.shape[0] // C

    def rope(t):
        return t * cos + pltpu.roll(t, hdk - RET_DK // 2, 1) * sa + pltpu.roll(t, RET_DK // 2, 1) * sb

    qf = rope(q_in.astype(F32)) * (RET_DK ** -0.5)
    kf = rope(k_in.astype(F32))
    q = qf.astype(BF16)
    k = kf.astype(BF16)
    lane = lax.broadcasted_iota(jnp.int32, (C, pw), 1)
    keep0 = jnp.where(lane < RET_DK, 1.0, 0.0).astype(BF16)
    keep1 = jnp.where(lane >= RET_DK, 1.0, 0.0).astype(BF16)
    row = lax.broadcasted_iota(jnp.int32, (pw, vw), 0)
    col = lax.broadcasted_iota(jnp.int32, (pw, vw), 1)
    top = row < RET_DK
    diag = top == (col < RET_DV)
    zeros_v = jnp.zeros((C, RET_DV), BF16)
    nt_dims = (((1,), (1,)), ((), ()))
    tn_dims = (((0,), (0,)), ((), ()))

    scores = []
    for ci in range(nch):
        rs = slice(ci * C, (ci + 1) * C)
        for p in range(npair):
            kp = k[rs, p * pw:(p + 1) * pw]
            kcat = jnp.concatenate([kp * keep0, kp * keep1], axis=0)
            s = lax.dot_general(q[rs, p * pw:(p + 1) * pw], kcat, nt_dims, preferred_element_type=F32)
            scores.append((s * dmat_ref[p]).astype(BF16))

    st = [state[p] for p in range(npair)]
    before = []
    for ci in range(nch):
        rs = slice(ci * C, (ci + 1) * C)
        kz = (kf[rs] * zeta_ref[...]).astype(BF16)
        for p in range(npair):
            before.append(st[p].astype(BF16))
            upd = lax.dot_general(kz[:, p * pw:(p + 1) * pw], v_of(rs, slice(p * vw, (p + 1) * vw)), tn_dims,
                                  preferred_element_type=F32)
            st[p] = jnp.where(top, g_chunk[2 * p], g_chunk[2 * p + 1]) * st[p] + jnp.where(diag, upd, 0.0)
    for p in range(npair):
        state[p] = st[p]

    for ci in range(nch):
        rs = slice(ci * C, (ci + 1) * C)
        qx = (qf[rs] * xi_ref[...]).astype(BF16)
        for p in range(npair):
            vp = v_of(rs, slice(p * vw, (p + 1) * vw))
            vbd = jnp.concatenate([jnp.concatenate([vp[:, :RET_DV], zeros_v], axis=1),
                                   jnp.concatenate([zeros_v, vp[:, RET_DV:]], axis=1)], axis=0)
            lhs = jnp.concatenate([scores[ci * npair + p], qx[:, p * pw:(p + 1) * pw]], axis=1)
            rhs = jnp.concatenate([vbd, before[ci * npair + p]], axis=0)
            o2 = jnp.dot(lhs, rhs, preferred_element_type=F32)
            for e in range(2):
                vs = slice((2 * p + e) * RET_DV, (2 * p + e + 1) * RET_DV)
                o = o2[:, e * RET_DV:(e + 1) * RET_DV]
                mu = jnp.mean(o, axis=-1, keepdims=True)
                var = jnp.mean(jnp.square(o - mu), axis=-1, keepdims=True)
                o = (o - mu) * lax.rsqrt(var + EPS)
                store(rs, vs, (_silu(gr_of(rs, vs).astype(F32)) * o).astype(BF16))


def _retention_tables(S):
    C, H = RET_CHUNK, RET_HEADS
    inv_freq = ROPE_BASE ** (-jnp.arange(0, RET_DK, 2, dtype=F32) / RET_DK)
    ang = jnp.arange(S, dtype=F32)[:, None] * inv_freq[None, :]
    cos, sin = jnp.cos(ang), jnp.sin(ang)
    zero = jnp.zeros_like(sin)
    cos_f = jnp.tile(cos, (1, 2 * H))
    sin_a = jnp.tile(jnp.concatenate([-sin, zero], axis=1), (1, H))
    sin_b = jnp.tile(jnp.concatenate([zero, sin], axis=1), (1, H))
    log_g = jnp.log(1.0 - jnp.exp2(-5.0 - jnp.arange(H, dtype=F32)))
    pos = jnp.arange(C, dtype=F32)
    diff = pos[:, None] - pos[None, :]
    dmat = jnp.where(diff[None] >= 0, jnp.exp(jnp.maximum(diff, 0.0)[None] * log_g[:, None, None]), 0.0)
    xi = jnp.exp((pos + 1.0)[None] * log_g[:, None])
    zeta = jnp.exp((C - 1.0 - pos)[None] * log_g[:, None])
    xi_f = jnp.repeat(xi.T, RET_DK, axis=1)
    zeta_f = jnp.repeat(zeta.T, RET_DK, axis=1)
    g_chunk = tuple(float(np.exp(C * np.log(1.0 - 2.0 ** (-5.0 - h)))) for h in range(H))
    dmat2 = dmat.reshape(H // 2, 2, C, C).transpose(0, 2, 1, 3).reshape(H // 2, C, 2 * C)
    return (cos_f, sin_a, sin_b, dmat2, xi_f, zeta_f), g_chunk


def _resident(shape):
    return pl.BlockSpec(shape, lambda *_: (0,) * len(shape), pipeline_mode=pl.Buffered(1))


def _mixer_kernel(x_ref, u_ref, halo_ref, cos_ref, sa_ref, sb_ref, dmat_ref, xi_ref, zeta_ref,
                  cw_ref, cb_ref, lg_ref, lb_ref, wc_ref, wr_ref, wo_ref, o_ref, state, og, zbuf, cbuf, *, g_chunk):
    @pl.when(pl.program_id(1) == 0)
    def _():
        state[...] = jnp.zeros_like(state)

    nparts, rows, cc = cbuf.shape[0], cbuf.shape[1], cbuf.shape[2]
    d = x_ref.shape[2]
    hdk, hdv = RET_HEADS * RET_DK, RET_HEADS * RET_DV
    q0 = 2 * cc
    k0, v0 = q0 + hdk, q0 + 2 * hdk
    gr0 = v0 + hdv
    ga0 = gr0 + hdv
    gb0 = ga0 + d
    halo_z = jnp.where(pl.program_id(1) == 0, 0.0, _glu(halo_ref[0]))
    for part in range(nparts):
        r0 = part * rows
        rs = slice(r0, r0 + rows)
        shift = lambda sl, off: slice(sl.start + off, sl.stop + off)
        _retention_rows(
            u_ref[0, rs, q0:k0], u_ref[0, rs, k0:v0],
            lambda r, c: u_ref[0, shift(r, r0), shift(c, v0)],
            lambda r, c: u_ref[0, shift(r, r0), shift(c, gr0)],
            cos_ref[rs, :], sa_ref[rs, :], sb_ref[rs, :], dmat_ref, xi_ref, zeta_ref,
            lambda r, c, val: og.__setitem__((shift(r, r0), c), val), state, g_chunk)
        if part:
            halo_z = _glu(u_ref[0, r0 - CONV_HALO:r0, :q0])
        c = _conv_swish(u_ref[0, rs, :q0], halo_z, cw_ref, cb_ref, lg_ref, lb_ref, zbuf.at[part], cbuf.at[part])
        yc = jnp.dot(c, wc_ref[...], preferred_element_type=F32)
        yr = jnp.dot(og[rs, :], wr_ref[...], preferred_element_type=F32)
        merged = (jax.nn.sigmoid(u_ref[0, rs, ga0:gb0].astype(F32)) * yc
                  + jax.nn.sigmoid(u_ref[0, rs, gb0:gb0 + d].astype(F32)) * yr)
        o_ref[0, rs, :] = x_ref[0, rs, :] + jnp.dot(merged.astype(BF16), wo_ref[...], preferred_element_type=F32)


def _mixer(x3, u3, tables, g_chunk, conv_w, conv_b, ln_g, ln_b, wc, wr, wo, tm=512, nparts=2):
    B, S, D = x3.shape
    cc = conv_w.shape[1]
    ucols = u3.shape[2]
    tm = min(tm, S)
    hb = tm // CONV_HALO
    hdk, hdv = RET_HEADS * RET_DK, RET_HEADS * RET_DV
    C = RET_CHUNK
    seq = lambda cols: pl.BlockSpec((tm, cols), lambda b, i: (i, 0))
    return pl.pallas_call(
        functools.partial(_mixer_kernel, g_chunk=g_chunk),
        grid=(B, S // tm),
        in_specs=[pl.BlockSpec((1, tm, D), lambda b, i: (b, i, 0)),
                  pl.BlockSpec((1, tm, ucols), lambda b, i: (b, i, 0)),
                  pl.BlockSpec((1, CONV_HALO, 2 * cc), lambda b, i: (b, jnp.maximum(i * hb - 1, 0), 0)),
                  seq(hdk), seq(hdk), seq(hdk),
                  _resident((RET_HEADS // 2, C, 2 * C)), _resident((C, hdk)), _resident((C, hdk)),
                  _resident((CONV_WIDTH, cc)), _resident((1, cc)), _resident((1, cc)), _resident((1, cc)),
                  _resident((cc, D)), _resident((hdv, D)), _resident((D, D))],
        out_specs=pl.BlockSpec((1, tm, D), lambda b, i: (b, i, 0)),
        out_shape=jax.ShapeDtypeStruct((B, S, D), F32),
        scratch_shapes=[pltpu.VMEM((RET_HEADS // 2, 2 * RET_DK, 2 * RET_DV), F32),
                        pltpu.VMEM((tm, hdv), BF16),
                        pltpu.VMEM((nparts, tm // nparts + CONV_HALO + 2 * SUBLANES, cc), F32),
                        pltpu.VMEM((nparts, tm // nparts, cc), F32)],
        compiler_params=_cparams("parallel", "arbitrary"),
        name="mixer",
    )(x3, u3, u3, *tables, conv_w, conv_b.reshape(1, cc), ln_g.reshape(1, cc), ln_b.reshape(1, cc), wc, wr, wo)


class _Ple:
    def __init__(self, p, layer, g, wgate, wproj, g_final, final):
        D = wgate.shape[0]
        self.final = final
        self.operands = (p, g.reshape(1, D), wgate, wproj, g_final.reshape(1, D))
        self.layer, self.pdim, self.d = layer, p.shape[2], D

    def specs(self, rows, row_map):
        return [pl.BlockSpec((1, rows, self.pdim), lambda *a: (self.layer, row_map(*a), 0)),
                _resident((1, self.d)), _resident((self.d, self.d)), _resident((self.pdim, self.d)),
                _resident((1, self.d))]


def _ple_tail(x, p_ref, g_ref, wgate_ref, wproj_ref, gf_ref, final):
    h = _rms(x, g_ref[...]).astype(BF16)
    gate = jax.nn.sigmoid(jnp.dot(h, wgate_ref[...], preferred_element_type=F32))
    proj = jnp.dot(p_ref[0].astype(BF16), wproj_ref[...], preferred_element_type=F32)
    y = x + gate * proj
    return _rms(y, gf_ref[...]) if final else y


def _dense_ffn_kernel(x_ref, g_ref, wg_ref, wu_ref, wd_ref, p_ref, gp_ref, wpg_ref, wpp_ref, gf_ref, o_ref, *, final):
    x = x_ref[...]
    h = _rms(x, g_ref[...]).astype(BF16)
    a = _silu(jnp.dot(h, wg_ref[...], preferred_element_type=F32)) * jnp.dot(h, wu_ref[...], preferred_element_type=F32)
    x = x + jnp.dot(a.astype(BF16), wd_ref[...], preferred_element_type=F32)
    o_ref[...] = _ple_tail(x, p_ref, gp_ref, wpg_ref, wpp_ref, gf_ref, final)


def _dense_ffn(x, g, wg, wu, wd, ple, tm=512):
    T, D = x.shape
    F = wg.shape[1]
    tm = min(tm, T)
    return pl.pallas_call(
        functools.partial(_dense_ffn_kernel, final=ple.final),
        grid=(T // tm,),
        in_specs=[pl.BlockSpec((tm, D), lambda i: (i, 0)),
                  _resident((1, D)), _resident((D, F)), _resident((D, F)), _resident((F, D))]
                 + ple.specs(tm, lambda i: i),
        out_specs=pl.BlockSpec((tm, D), lambda i: (i, 0)),
        out_shape=jax.ShapeDtypeStruct((T, D), F32),
        compiler_params=_cparams("parallel"),
        name="dense_ffn_ple",
    )(x, g.reshape(1, D), wg, wu, wd, *ple.operands)


META_ROWS = 8


def _router_kernel(x_ref, g_ref, wr_ref, meta_ref, cnt_ref, run_ref):
    @pl.when(pl.program_id(0) == 0)
    def _():
        run_ref[...] = jnp.zeros_like(run_ref)

    h = _rms(x_ref[...], g_ref[...])
    h_hi = h.astype(BF16)
    h_lo = (h - h_hi.astype(F32)).astype(BF16)
    w_hi, w_lo = wr_ref[0], wr_ref[1]
    logits = (jnp.dot(h_hi, w_hi, preferred_element_type=F32) + jnp.dot(h_lo, w_hi, preferred_element_type=F32)
              + jnp.dot(h_hi, w_lo, preferred_element_type=F32))
    tt = logits.shape[0]
    lane = lax.broadcasted_iota(jnp.int32, (tt, LANES), 1)
    neg = jnp.float32(-jnp.inf)
    logits = jnp.where(lane < N_EXPERTS, logits, neg)
    m1 = jnp.max(logits, axis=1, keepdims=True)
    i1 = jnp.min(jnp.where(logits == m1, lane, LANES), axis=1, keepdims=True)
    mask1 = lane == i1
    rest = jnp.where(mask1, neg, logits)
    m2 = jnp.max(rest, axis=1, keepdims=True)
    i2 = jnp.min(jnp.where(rest == m2, lane, LANES), axis=1, keepdims=True)
    mask2 = lane == i2
    d = jnp.exp(m2 - m1)
    g1 = 1.0 / (1.0 + d)
    g2 = d / (1.0 + d)
    onehot = jnp.where(mask1 | mask2, 1.0, 0.0)
    row = lax.broadcasted_iota(jnp.int32, (tt, tt), 0)
    col = lax.broadcasted_iota(jnp.int32, (tt, tt), 1)
    lower = jnp.where(col < row, 1.0, 0.0).astype(BF16)
    excl = jnp.dot(lower, onehot.astype(BF16), preferred_element_type=F32) + run_ref[...]
    r1 = jnp.sum(jnp.where(mask1, excl, 0.0), axis=1, keepdims=True).astype(jnp.int32)
    r2 = jnp.sum(jnp.where(mask2, excl, 0.0), axis=1, keepdims=True).astype(jnp.int32)
    run_ref[...] += jnp.sum(onehot, axis=0, keepdims=True)
    cnt_ref[0] = run_ref[...].astype(jnp.int32)
    gb1 = pltpu.bitcast(jnp.broadcast_to(g1, (tt, LANES)), jnp.int32)
    gb2 = pltpu.bitcast(jnp.broadcast_to(g2, (tt, LANES)), jnp.int32)
    rec = jnp.where(lane == 0, i1, jnp.where(lane == 1, i2, jnp.where(lane == 2, r1, jnp.where(
        lane == 3, r2, jnp.where(lane == 4, gb1, jnp.where(lane == 5, gb2, 0))))))
    meta_ref[0] = rec.T[:META_ROWS, :]


def _router(x, g, w_router):
    T, D = x.shape
    nt = T // TOK_TILE
    wr = jnp.zeros((D, LANES), F32).at[:, :N_EXPERTS].set(w_router)
    wr_hi = wr.astype(BF16)
    wr = jnp.stack([wr_hi, (wr - wr_hi.astype(F32)).astype(BF16)])
    return pl.pallas_call(
        _router_kernel,
        grid=(nt,),
        in_specs=[pl.BlockSpec((TOK_TILE, D), lambda s: (s, 0)),
                  pl.BlockSpec((1, D), lambda s: (0, 0)),
                  pl.BlockSpec((2, D, LANES), lambda s: (0, 0, 0))],
        out_specs=[pl.BlockSpec((1, META_ROWS, TOK_TILE), lambda s: (s, 0, 0)),
                   pl.BlockSpec((1, 1, LANES), lambda s: (s, 0, 0))],
        out_shape=[jax.ShapeDtypeStruct((nt, META_ROWS, TOK_TILE), jnp.int32),
                   jax.ShapeDtypeStruct((nt, 1, LANES), jnp.int32)],
        scratch_shapes=[pltpu.VMEM((1, LANES), F32)],
        compiler_params=_cparams("arbitrary"),
        name="router",
    )(x, g.reshape(1, D), wr)


def _dispatch_kernel(pad_ref, dest_ref, x_ref, g_ref, wg_ref, wu_ref, wd_ref,
                     xin_hbm, wgo_ref, wuo_ref, wdo_ref, hbuf, sem, zsem, *, nt):
    s = pl.program_id(0)
    slot = s % 2
    hbuf[slot] = _rms(x_ref[...], g_ref[...])

    for t in range(TOK_TILE):
        for k in range(2):
            pltpu.make_async_copy(hbuf.at[slot, pl.ds(t, 1)],
                                  xin_hbm.at[pl.ds(dest_ref[0, 0, k * TOK_TILE + t], 1)],
                                  sem.at[slot]).start(priority=k)

    for w_ref, wo_ref in ((wg_ref, wgo_ref), (wu_ref, wuo_ref), (wd_ref, wdo_ref)):
        wo_ref[...] = w_ref[0].astype(BF16)

    def drain(sl):
        for _ in range(2):
            pltpu.make_async_copy(hbuf.at[sl], xin_hbm.at[pl.ds(0, TOK_TILE)], sem.at[sl]).wait()

    @pl.when(s > 0)
    def _():
        drain(1 - slot)

    @pl.when(s == nt - 1)
    def _():
        drain(slot)
        hbuf[0] = jnp.zeros(hbuf.shape[1:], F32)

        def zissue(r, c):
            pltpu.make_async_copy(hbuf.at[0, pl.ds(0, 1)], xin_hbm.at[pl.ds(r, 1)], zsem).start()
            return c

        def zwait(r, c):
            pltpu.make_async_copy(hbuf.at[0, pl.ds(0, 1)], xin_hbm.at[pl.ds(0, 1)], zsem).wait()
            return c

        for e in range(N_EXPERTS):
            lo, hi = pad_ref[e], pad_ref[N_EXPERTS + e]
            lax.fori_loop(lo, hi, zissue, 0)
            lax.fori_loop(lo, hi, zwait, 0)

        def tissue(r, c):
            pltpu.make_async_copy(hbuf.at[0], xin_hbm.at[pl.ds(pl.multiple_of(r * TOK_TILE, TOK_TILE), TOK_TILE)],
                                  zsem).start()
            return c

        def twait(r, c):
            pltpu.make_async_copy(hbuf.at[0], xin_hbm.at[pl.ds(0, TOK_TILE)], zsem).wait()
            return c

        tail_lo, tail_hi = pad_ref[2 * N_EXPERTS - 1] // TOK_TILE, xin_hbm.shape[0] // TOK_TILE
        lax.fori_loop(tail_lo, tail_hi, tissue, 0)
        lax.fori_loop(tail_lo, tail_hi, twait, 0)


def _dispatch(x, g, dest, pad_info, n_rows, layer, w_gate, w_up, w_down):
    T, D = x.shape
    nt = T // TOK_TILE
    _, E, K, F = w_gate.shape
    flat = lambda w: w.reshape(w.shape[0], E * w.shape[2], w.shape[3])
    ck, cf = E * K // nt, E * F // nt
    w_in_spec = lambda rows, cols: pl.BlockSpec((1, rows, cols), lambda s, pad: (layer, s, 0))
    w_out_spec = lambda rows, cols: pl.BlockSpec((rows, cols), lambda s, pad: (s, 0))
    xin, wg, wu, wd = pl.pallas_call(
        functools.partial(_dispatch_kernel, nt=nt),
        grid_spec=pltpu.PrefetchScalarGridSpec(
            num_scalar_prefetch=1,
            grid=(nt,),
            in_specs=[pl.BlockSpec((1, 1, 2 * TOK_TILE), lambda s, pad: (s, 0, 0), memory_space=pltpu.SMEM),
                      pl.BlockSpec((TOK_TILE, D), lambda s, pad: (s, 0)),
                      pl.BlockSpec((1, D), lambda s, pad: (0, 0)),
                      w_in_spec(ck, F), w_in_spec(ck, F), w_in_spec(cf, K)],
            out_specs=[pl.BlockSpec(memory_space=pl.ANY),
                       w_out_spec(ck, F), w_out_spec(ck, F), w_out_spec(cf, K)],
            scratch_shapes=[pltpu.VMEM((2, TOK_TILE, D), F32),
                            pltpu.SemaphoreType.DMA((2,)), pltpu.SemaphoreType.DMA(())]),
        out_shape=[jax.ShapeDtypeStruct((n_rows, D), F32),
                   jax.ShapeDtypeStruct((E * K, F), BF16), jax.ShapeDtypeStruct((E * K, F), BF16),
                   jax.ShapeDtypeStruct((E * F, K), BF16)],
        compiler_params=_cparams("arbitrary"),
        name="moe_dispatch",
    )(pad_info, dest, x, g.reshape(1, D), flat(w_gate), flat(w_up), flat(w_down))
    return xin, wg.reshape(E, K, F), wu.reshape(E, K, F), wd.reshape(E, F, K)


def _moe_ffn_kernel(be_ref, bv_ref, x_ref, wg_ref, wu_ref, wd_ref, o_ref, xb_ref):
    b, f = pl.program_id(0), pl.program_id(1)

    @pl.when(f == 0)
    def _():
        xb_ref[...] = x_ref[...].astype(BF16)
        o_ref[...] = jnp.zeros_like(o_ref)

    @pl.when(bv_ref[b] != 0)
    def _():
        x = xb_ref[...]
        a = _silu(jnp.dot(x, wg_ref[0], preferred_element_type=F32)) * jnp.dot(x, wu_ref[0], preferred_element_type=F32)
        o_ref[...] += jnp.dot(a.astype(BF16), wd_ref[0], preferred_element_type=F32)


def _moe_ffn(xin, wg, wu, wd, block_e, block_valid, tf=1792):
    R, D = xin.shape
    F = wg.shape[2]
    nb, nf = R // ROW_BLK, F // tf
    fsel = lambda b, f, bv: jnp.where(bv[b] != 0, f, nf - 1)
    return pl.pallas_call(
        _moe_ffn_kernel,
        grid_spec=pltpu.PrefetchScalarGridSpec(
            num_scalar_prefetch=2,
            grid=(nb, nf),
            in_specs=[pl.BlockSpec((ROW_BLK, D), lambda b, f, be, bv: (jnp.where(bv[b] != 0, b, 0), 0)),
                      pl.BlockSpec((1, D, tf), lambda b, f, be, bv: (be[b], 0, fsel(b, f, bv))),
                      pl.BlockSpec((1, D, tf), lambda b, f, be, bv: (be[b], 0, fsel(b, f, bv))),
                      pl.BlockSpec((1, tf, D), lambda b, f, be, bv: (be[b], fsel(b, f, bv), 0))],
            out_specs=pl.BlockSpec((ROW_BLK, D), lambda b, f, be, bv: (b, 0)),
            scratch_shapes=[pltpu.VMEM((ROW_BLK, D), BF16)]),
        out_shape=jax.ShapeDtypeStruct((R, D), F32),
        compiler_params=_cparams("arbitrary", "arbitrary"),
        name="moe_ffn",
    )(block_e, block_valid, xin, wg, wu, wd)


def _combine_kernel(dcur_ref, dnext_ref, x_ref, g1_ref, g2_ref, y_hbm, p_ref, gp_ref, wpg_ref, wpp_ref, gf_ref,
                    o_ref, ya, yb, sem, *, nt, final):
    s = pl.program_id(0)
    slot = s % 2

    def issue(dref, sl, t):
        pltpu.make_async_copy(y_hbm.at[pl.ds(dref[0, 0, t], 1)], ya.at[sl, pl.ds(t, 1)], sem.at[sl]).start()
        pltpu.make_async_copy(y_hbm.at[pl.ds(dref[0, 0, TOK_TILE + t], 1)], yb.at[sl, pl.ds(t, 1)],
                              sem.at[sl]).start(priority=1)

    def drain(sl):
        for buf in (ya, yb):
            pltpu.make_async_copy(y_hbm.at[pl.ds(0, TOK_TILE)], buf.at[sl], sem.at[sl]).wait()

    @pl.when(s == 0)
    def _():
        lax.fori_loop(0, TOK_TILE, lambda t, c: issue(dcur_ref, slot, t) or c, 0, unroll=8)

    drain(slot)
    for t in range(TOK_TILE):
        issue(dnext_ref, 1 - slot, t)
    x = x_ref[...] + g1_ref[...] * ya[slot] + g2_ref[...] * yb[slot]
    o_ref[...] = _ple_tail(x, p_ref, gp_ref, wpg_ref, wpp_ref, gf_ref, final)

    @pl.when(s == nt - 1)
    def _():
        drain(1 - slot)


def _combine(x, y, dest, g1, g2, ple):
    T, D = x.shape
    nt = T // TOK_TILE
    smem_spec = lambda imap: pl.BlockSpec((1, 1, 2 * TOK_TILE), imap, memory_space=pltpu.SMEM)
    col_spec = pl.BlockSpec((TOK_TILE, 1), lambda s: (s, 0))
    return pl.pallas_call(
        functools.partial(_combine_kernel, nt=nt, final=ple.final),
        grid=(nt,),
        in_specs=[smem_spec(lambda s: (s, 0, 0)),
                  smem_spec(lambda s: (jnp.minimum(s + 1, nt - 1), 0, 0)),
                  pl.BlockSpec((TOK_TILE, D), lambda s: (s, 0)),
                  col_spec, col_spec,
                  pl.BlockSpec(memory_space=pl.ANY)] + ple.specs(TOK_TILE, lambda s: s),
        out_specs=pl.BlockSpec((TOK_TILE, D), lambda s: (s, 0)),
        out_shape=jax.ShapeDtypeStruct((T, D), F32),
        scratch_shapes=[pltpu.VMEM((2, TOK_TILE, D), F32), pltpu.VMEM((2, TOK_TILE, D), F32),
                        pltpu.SemaphoreType.DMA((2,))],
        compiler_params=_cparams("arbitrary"),
        name="moe_combine_ple",
    )(dest, dest, x, g1.reshape(T, 1), g2.reshape(T, 1), y, *ple.operands)


def _moe_layer(x, g, w_router, layer, w_gate, w_up, w_down, ple):
    T, D = x.shape
    nt = T // TOK_TILE
    n_blocks = (2 * T + N_EXPERTS * (ROW_BLK - 1) + ROW_BLK - 1) // ROW_BLK
    meta, cnt = _router(x, g, w_router)
    e1, e2, r1, r2 = meta[:, 0], meta[:, 1], meta[:, 2], meta[:, 3]
    g1 = lax.bitcast_convert_type(meta[:, 4], F32)
    g2 = lax.bitcast_convert_type(meta[:, 5], F32)
    counts = cnt[-1, 0, :N_EXPERTS]
    padded = ((counts + ROW_BLK - 1) // ROW_BLK) * ROW_BLK
    pad_end = jnp.cumsum(padded)
    pad_start = pad_end - padded
    dest = jnp.concatenate([pad_start[e1] + r1, pad_start[e2] + r2], axis=1).reshape(nt, 1, 2 * TOK_TILE)
    pad_info = jnp.concatenate([pad_start + counts, pad_end]).astype(jnp.int32)
    xin, wg, wu, wd = _dispatch(x, g, dest, pad_info, n_blocks * ROW_BLK, layer, w_gate, w_up, w_down)
    bstart = jnp.arange(n_blocks, dtype=jnp.int32) * ROW_BLK
    block_e = jnp.minimum(jnp.sum(bstart[:, None] >= pad_end[None, :], axis=1), N_EXPERTS - 1).astype(jnp.int32)
    block_valid = (bstart < pad_end[-1]).astype(jnp.int32)
    y = _moe_ffn(xin, wg, wu, wd, block_e, block_valid)
    return _combine(x, y, dest, g1.reshape(T), g2.reshape(T), ple)


def kernel(x, p, g_mix, w_in, conv_w, conv_b, conv_ln_g, conv_ln_b, w_conv_out, w_ret_out, w_out, g_ffn, w_dense_gate, w_dense_up, w_dense_down, w_router, w_exp_gate, w_exp_up, w_exp_down, g_ple, w_ple_gate, w_ple_proj, g_final):
    B, S, D = x.shape
    T = B * S
    depth = w_in.shape[0]
    tables, g_chunk = _retention_tables(S)
    bf = lambda w: w.astype(BF16)
    xf = x.reshape(T, D)
    pf = p.reshape(depth, T, p.shape[-1])
    for i in range(depth):
        u = _inproj(xf, g_mix[i], bf(w_in[i]))
        u3 = u.reshape(B, S, u.shape[1])
        xf = _mixer(xf.reshape(B, S, D), u3, tables, g_chunk, conv_w[i], conv_b[i], conv_ln_g[i], conv_ln_b[i],
                    bf(w_conv_out[i]), bf(w_ret_out[i]), bf(w_out[i])).reshape(T, D)
        j = i // 2
        ple = _Ple(pf, i, g_ple[i], bf(w_ple_gate[i]), bf(w_ple_proj[i]), g_final, i == depth - 1)
        if i % 2 == 0:
            xf = _dense_ffn(xf, g_ffn[i], bf(w_dense_gate[j]), bf(w_dense_up[j]), bf(w_dense_down[j]), ple)
        else:
            xf = _moe_layer(xf, g_ffn[i], w_router[j], j, w_exp_gate, w_exp_up, w_exp_down, ple)
    return xf.reshape(B, S, D)
```

```python
import functools

import numpy as np
import jax
import jax.numpy as jnp
from jax import lax
from jax.experimental import pallas as pl
from jax.experimental.pallas import tpu as pltpu

F32 = jnp.float32
BF16 = jnp.bfloat16

EPS = 1e-6
CONV_WIDTH = 31
RET_HEADS = 8
RET_DK = 64
RET_DV = 128
RET_CHUNK = 128
ROPE_BASE = 10000.0
N_EXPERTS = 8

V7X_VMEM_LIMIT_BYTES = 56 * 1024 * 1024
LANES = 128
SUBLANES = 8
CONV_HALO = 32
TOK_TILE = 512
ROW_BLK = 512


def _cparams(*sem):
    return pltpu.CompilerParams(dimension_semantics=sem, vmem_limit_bytes=V7X_VMEM_LIMIT_BYTES)


def _rms(x, g):
    return x * lax.rsqrt(jnp.mean(x * x, axis=-1, keepdims=True) + EPS) * g


def _silu(x):
    return x * jax.nn.sigmoid(x)


def _inproj_kernel(x_ref, g_ref, w_ref, o_ref, h_ref):
    @pl.when(pl.program_id(1) == 0)
    def _():
        h_ref[...] = _rms(x_ref[...], g_ref[...]).astype(BF16)

    o_ref[...] = jnp.dot(h_ref[...], w_ref[...], preferred_element_type=F32).astype(BF16)


def _inproj(x, g, w, tm=1024, tn=3072):
    T, D = x.shape
    N = w.shape[1]
    tm = min(tm, T)
    return pl.pallas_call(
        _inproj_kernel,
        grid=(T // tm, N // tn),
        in_specs=[pl.BlockSpec((tm, D), lambda i, j: (i, 0)),
                  pl.BlockSpec((1, D), lambda i, j: (0, 0)),
                  pl.BlockSpec((D, tn), lambda i, j: (0, j))],
        out_specs=pl.BlockSpec((tm, tn), lambda i, j: (i, j)),
        out_shape=jax.ShapeDtypeStruct((T, N), BF16),
        scratch_shapes=[pltpu.VMEM((tm, D), BF16)],
        compiler_params=_cparams("parallel", "arbitrary"),
        name="inproj",
    )(x, g.reshape(1, D), w)


CONV_ROW_CHUNK = 128


def _glu(t):
    half = t.shape[1] // 2
    t = t.astype(F32)
    return t[:, :half] * jax.nn.sigmoid(t[:, half:])


def _conv_swish(cur, halo_z, w_ref, b_ref, lg_ref, lb_ref, zbuf, cbuf):
    ts, cc, rc = cbuf.shape[0], cbuf.shape[1], CONV_ROW_CHUNK
    zbuf[CONV_HALO:CONV_HALO + ts, :] = _glu(cur)
    zbuf[:CONV_HALO, :] = halo_z
    zbuf[CONV_HALO + ts:, :] = jnp.zeros((2 * SUBLANES, cc), F32)
    first = CONV_HALO - (CONV_WIDTH - 1)
    for c in range(cc // LANES):
        cs = slice(c * LANES, (c + 1) * LANES)
        for r in range(ts // rc):
            acc = jnp.broadcast_to(b_ref[:, cs], (rc, LANES))
            for b in range(SUBLANES):
                part = None
                for a in range((first + CONV_WIDTH - 1) // SUBLANES + 1):
                    j = SUBLANES * a + b - first
                    if 0 <= j < CONV_WIDTH:
                        lo = r * rc + SUBLANES * a
                        term = w_ref[j:j + 1, cs] * zbuf[lo:lo + rc + SUBLANES, cs]
                        part = term if part is None else part + term
                acc = acc + part[b:b + rc]
            cbuf[r * rc:(r + 1) * rc, cs] = acc
    z = cbuf[...]
    mu = jnp.mean(z, axis=-1, keepdims=True)
    var = jnp.mean(jnp.square(z - mu), axis=-1, keepdims=True)
    y = (z - mu) * lax.rsqrt(var + EPS) * lg_ref[...] + lb_ref[...]
    return _silu(y).astype(BF16)


def _retention_rows(q_in, k_in, v_of, gr_of, cos, sa, sb, dmat_ref, xi_ref, zeta_ref, store, state, g_chunk):
    C, pw, vw = RET_CHUNK, 2 * RET_DK, 2 * RET_DV
    hdk = RET_HEADS * RET_DK
    npair = RET_HEADS // 2
    nch = q_in.shape[0] // C

    def rope(t):
        return t * cos + pltpu.roll(t, hdk - RET_DK // 2, 1) * sa + pltpu.roll(t, RET_DK // 2, 1) * sb

    qf = rope(q_in.astype(F32)) * (RET_DK ** -0.5)
    kf = rope(k_in.astype(F32))
    q = qf.astype(BF16)
    k = kf.astype(BF16)
    lane = lax.broadcasted_iota(jnp.int32, (C, pw), 1)
    keep0 = jnp.where(lane < RET_DK, 1.0, 0.0).astype(BF16)
    keep1 = jnp.where(lane >= RET_DK, 1.0, 0.0).astype(BF16)
    row = lax.broadcasted_iota(jnp.int32, (pw, vw), 0)
    col = lax.broadcasted_iota(jnp.int32, (pw, vw), 1)
    top = row < RET_DK
    diag = top == (col < RET_DV)
    zeros_v = jnp.zeros((C, RET_DV), BF16)
    nt_dims = (((1,), (1,)), ((), ()))
    tn_dims = (((0,), (0,)), ((), ()))

    scores = []
    for ci in range(nch):
        rs = slice(ci * C, (ci + 1) * C)
        for p in range(npair):
            kp = k[rs, p * pw:(p + 1) * pw]
            kcat = jnp.concatenate([kp * keep0, kp * keep1], axis=0)
            s = lax.dot_general(q[rs, p * pw:(p + 1) * pw], kcat, nt_dims, preferred_element_type=F32)
            scores.append((s * dmat_ref[p]).astype(BF16))

    st = [state[p] for p in range(npair)]
    before = []
    for ci in range(nch):
        rs = slice(ci * C, (ci + 1) * C)
        kz = (kf[rs] * zeta_ref[...]).astype(BF16)
        for p in range(npair):
            before.append(st[p].astype(BF16))
            upd = lax.dot_general(kz[:, p * pw:(p + 1) * pw], v_of(rs, slice(p * vw, (p + 1) * vw)), tn_dims,
                                  preferred_element_type=F32)
            st[p] = jnp.where(top, g_chunk[2 * p], g_chunk[2 * p + 1]) * st[p] + jnp.where(diag, upd, 0.0)
    for p in range(npair):
        state[p] = st[p]

    for ci in range(nch):
        rs = slice(ci * C, (ci + 1) * C)
        qx = (qf[rs] * xi_ref[...]).astype(BF16)
        for p in range(npair):
            vp = v_of(rs, slice(p * vw, (p + 1) * vw))
            vbd = jnp.concatenate([jnp.concatenate([vp[:, :RET_DV], zeros_v], axis=1),
                                   jnp.concatenate([zeros_v, vp[:, RET_DV:]], axis=1)], axis=0)
            lhs = jnp.concatenate([scores[ci * npair + p], qx[:, p * pw:(p + 1) * pw]], axis=1)
            rhs = jnp.concatenate([vbd, before[ci * npair + p]], axis=0)
            o2 = jnp.dot(lhs, rhs, preferred_element_type=F32)
            for e in range(2):
                vs = slice((2 * p + e) * RET_DV, (2 * p + e + 1) * RET_DV)
                o = o2[:, e * RET_DV:(e + 1) * RET_DV]
                mu = jnp.mean(o, axis=-1, keepdims=True)
                var = jnp.mean(jnp.square(o - mu), axis=-1, keepdims=True)
                o = (o - mu) * lax.rsqrt(var + EPS)
                store(rs, vs, (_silu(gr_of(rs, vs).astype(F32)) * o).astype(BF16))


def _retention_tables(S):
    C, H = RET_CHUNK, RET_HEADS
    inv_freq = ROPE_BASE ** (-jnp.arange(0, RET_DK, 2, dtype=F32) / RET_DK)
    ang = jnp.arange(S, dtype=F32)[:, None] * inv_freq[None, :]
    cos, sin = jnp.cos(ang), jnp.sin(ang)
    zero = jnp.zeros_like(sin)
    cos_f = jnp.tile(cos, (1, 2 * H))
    sin_a = jnp.tile(jnp.concatenate([-sin, zero], axis=1), (1, H))
    sin_b = jnp.tile(jnp.concatenate([zero, sin], axis=1), (1, H))
    log_g = jnp.log(1.0 - jnp.exp2(-5.0 - jnp.arange(H, dtype=F32)))
    pos = jnp.arange(C, dtype=F32)
    diff = pos[:, None] - pos[None, :]
    dmat = jnp.where(diff[None] >= 0, jnp.exp(jnp.maximum(diff, 0.0)[None] * log_g[:, None, None]), 0.0)
    xi = jnp.exp((pos + 1.0)[None] * log_g[:, None])
    zeta = jnp.exp((C - 1.0 - pos)[None] * log_g[:, None])
    xi_f = jnp.repeat(xi.T, RET_DK, axis=1)
    zeta_f = jnp.repeat(zeta.T, RET_DK, axis=1)
    g_chunk = tuple(float(np.exp(C * np.log(1.0 - 2.0 ** (-5.0 - h)))) for h in range(H))
    dmat2 = dmat.reshape(H // 2, 2, C, C).transpose(0, 2, 1, 3).reshape(H // 2, C, 2 * C)
    return (cos_f, sin_a, sin_b, dmat2, xi_f, zeta_f), g_chunk


def _resident(shape):
    return pl.BlockSpec(shape, lambda *_: (0,) * len(shape), pipeline_mode=pl.Buffered(1))


def _mixer_kernel(x_ref, u_ref, halo_ref, cos_ref, sa_ref, sb_ref, dmat_ref, xi_ref, zeta_ref,
                  cw_ref, cb_ref, lg_ref, lb_ref, wc_ref, wr_ref, wo_ref, o_ref, state, og, zbuf, cbuf, *, g_chunk):
    @pl.when(pl.program_id(1) == 0)
    def _():
        state[...] = jnp.zeros_like(state)

    nparts, rows, cc = cbuf.shape[0], cbuf.shape[1], cbuf.shape[2]
    d = x_ref.shape[2]
    hdk, hdv = RET_HEADS * RET_DK, RET_HEADS * RET_DV
    q0 = 2 * cc
    k0, v0 = q0 + hdk, q0 + 2 * hdk
    gr0 = v0 + hdv
    ga0 = gr0 + hdv
    gb0 = ga0 + d
    halo_z = jnp.where(pl.program_id(1) == 0, 0.0, _glu(halo_ref[0]))
    for part in range(nparts):
        r0 = part * rows
        rs = slice(r0, r0 + rows)
        shift = lambda sl, off: slice(sl.start + off, sl.stop + off)
        _retention_rows(
            u_ref[0, rs, q0:k0], u_ref[0, rs, k0:v0],
            lambda r, c: u_ref[0, shift(r, r0), shift(c, v0)],
            lambda r, c: u_ref[0, shift(r, r0), shift(c, gr0)],
            cos_ref[rs, :], sa_ref[rs, :], sb_ref[rs, :], dmat_ref, xi_ref, zeta_ref,
            lambda r, c, val: og.__setitem__((shift(r, r0), c), val), state, g_chunk)
        if part:
            halo_z = _glu(u_ref[0, r0 - CONV_HALO:r0, :q0])
        c = _conv_swish(u_ref[0, rs, :q0], halo_z, cw_ref, cb_ref, lg_ref, lb_ref, zbuf.at[part], cbuf.at[part])
        yc = jnp.dot(c, wc_ref[...], preferred_element_type=F32)
        yr = jnp.dot(og[rs, :], wr_ref[...], preferred_element_type=F32)
        merged = (jax.nn.sigmoid(u_ref[0, rs, ga0:gb0].astype(F32)) * yc
                  + jax.nn.sigmoid(u_ref[0, rs, gb0:gb0 + d].astype(F32)) * yr)
        o_ref[0, rs, :] = x_ref[0, rs, :] + jnp.dot(merged.astype(BF16), wo_ref[...], preferred_element_type=F32)


def _mixer(x3, u3, tables, g_chunk, conv_w, conv_b, ln_g, ln_b, wc, wr, wo, tm=512, nparts=2):
    B, S, D = x3.shape
    cc = conv_w.shape[1]
    ucols = u3.shape[2]
    tm = min(tm, S)
    hb = tm // CONV_HALO
    hdk, hdv = RET_HEADS * RET_DK, RET_HEADS * RET_DV
    C = RET_CHUNK
    seq = lambda cols: pl.BlockSpec((tm, cols), lambda b, i: (i, 0))
    return pl.pallas_call(
        functools.partial(_mixer_kernel, g_chunk=g_chunk),
        grid=(B, S // tm),
        in_specs=[pl.BlockSpec((1, tm, D), lambda b, i: (b, i, 0)),
                  pl.BlockSpec((1, tm, ucols), lambda b, i: (b, i, 0)),
                  pl.BlockSpec((1, CONV_HALO, 2 * cc), lambda b, i: (b, jnp.maximum(i * hb - 1, 0), 0)),
                  seq(hdk), seq(hdk), seq(hdk),
                  _resident((RET_HEADS // 2, C, 2 * C)), _resident((C, hdk)), _resident((C, hdk)),
                  _resident((CONV_WIDTH, cc)), _resident((1, cc)), _resident((1, cc)), _resident((1, cc)),
                  _resident((cc, D)), _resident((hdv, D)), _resident((D, D))],
        out_specs=pl.BlockSpec((1, tm, D), lambda b, i: (b, i, 0)),
        out_shape=jax.ShapeDtypeStruct((B, S, D), F32),
        scratch_shapes=[pltpu.VMEM((RET_HEADS // 2, 2 * RET_DK, 2 * RET_DV), F32),
                        pltpu.VMEM((tm, hdv), BF16),
                        pltpu.VMEM((nparts, tm // nparts + CONV_HALO + 2 * SUBLANES, cc), F32),
                        pltpu.VMEM((nparts, tm // nparts, cc), F32)],
        compiler_params=_cparams("parallel", "arbitrary"),
        name="mixer",
    )(x3, u3, u3, *tables, conv_w, conv_b.reshape(1, cc), ln_g.reshape(1, cc), ln_b.reshape(1, cc), wc, wr, wo)


class _Ple:
    def __init__(self, p, layer, g, wgate, wproj, g_final, final):
        D = wgate.shape[0]
        self.final = final
        self.operands = (p, g.reshape(1, D), wgate, wproj, g_final.reshape(1, D))
        self.layer, self.pdim, self.d = layer, p.shape[2], D

    def specs(self, rows, row_map):
        return [pl.BlockSpec((1, rows, self.pdim), lambda *a: (self.layer, row_map(*a), 0)),
                _resident((1, self.d)), _resident((self.d, self.d)), _resident((self.pdim, self.d)),
                _resident((1, self.d))]


def _ple_tail(x, p_ref, g_ref, wgate_ref, wproj_ref, gf_ref, final):
    h = _rms(x, g_ref[...]).astype(BF16)
    gate = jax.nn.sigmoid(jnp.dot(h, wgate_ref[...], preferred_element_type=F32))
    proj = jnp.dot(p_ref[0].astype(BF16), wproj_ref[...], preferred_element_type=F32)
    y = x + gate * proj
    return _rms(y, gf_ref[...]) if final else y


def _dense_ffn_kernel(x_ref, g_ref, wg_ref, wu_ref, wd_ref, p_ref, gp_ref, wpg_ref, wpp_ref, gf_ref, o_ref, *, final):
    x = x_ref[...]
    h = _rms(x, g_ref[...]).astype(BF16)
    a = _silu(jnp.dot(h, wg_ref[...], preferred_element_type=F32)) * jnp.dot(h, wu_ref[...], preferred_element_type=F32)
    x = x + jnp.dot(a.astype(BF16), wd_ref[...], preferred_element_type=F32)
    o_ref[...] = _ple_tail(x, p_ref, gp_ref, wpg_ref, wpp_ref, gf_ref, final)


def _dense_ffn(x, g, wg, wu, wd, ple, tm=512):
    T, D = x.shape
    F = wg.shape[1]
    tm = min(tm, T)
    return pl.pallas_call(
        functools.partial(_dense_ffn_kernel, final=ple.final),
        grid=(T // tm,),
        in_specs=[pl.BlockSpec((tm, D), lambda i: (i, 0)),
                  _resident((1, D)), _resident((D, F)), _resident((D, F)), _resident((F, D))]
                 + ple.specs(tm, lambda i: i),
        out_specs=pl.BlockSpec((tm, D), lambda i: (i, 0)),
        out_shape=jax.ShapeDtypeStruct((T, D), F32),
        compiler_params=_cparams("parallel"),
        name="dense_ffn_ple",
    )(x, g.reshape(1, D), wg, wu, wd, *ple.operands)


META_ROWS = 8


def _router_kernel(x_ref, g_ref, wr_ref, meta_ref, cnt_ref, run_ref):
    @pl.when(pl.program_id(0) == 0)
    def _():
        run_ref[...] = jnp.zeros_like(run_ref)

    h = _rms(x_ref[...], g_ref[...])
    h_hi = h.astype(BF16)
    h_lo = (h - h_hi.astype(F32)).astype(BF16)
    w_hi, w_lo = wr_ref[0], wr_ref[1]
    logits = (jnp.dot(h_hi, w_hi, preferred_element_type=F32) + jnp.dot(h_lo, w_hi, preferred_element_type=F32)
              + jnp.dot(h_hi, w_lo, preferred_element_type=F32))
    tt = logits.shape[0]
    lane = lax.broadcasted_iota(jnp.int32, (tt, LANES), 1)
    neg = jnp.float32(-jnp.inf)
    logits = jnp.where(lane < N_EXPERTS, logits, neg)
    m1 = jnp.max(logits, axis=1, keepdims=True)
    i1 = jnp.min(jnp.where(logits == m1, lane, LANES), axis=1, keepdims=True)
    mask1 = lane == i1
    rest = jnp.where(mask1, neg, logits)
    m2 = jnp.max(rest, axis=1, keepdims=True)
    i2 = jnp.min(jnp.where(rest == m2, lane, LANES), axis=1, keepdims=True)
    mask2 = lane == i2
    d = jnp.exp(m2 - m1)
    g1 = 1.0 / (1.0 + d)
    g2 = d / (1.0 + d)
    onehot = jnp.where(mask1 | mask2, 1.0, 0.0)
    row = lax.broadcasted_iota(jnp.int32, (tt, tt), 0)
    col = lax.broadcasted_iota(jnp.int32, (tt, tt), 1)
    lower = jnp.where(col < row, 1.0, 0.0).astype(BF16)
    excl = jnp.dot(lower, onehot.astype(BF16), preferred_element_type=F32) + run_ref[...]
    r1 = jnp.sum(jnp.where(mask1, excl, 0.0), axis=1, keepdims=True).astype(jnp.int32)
    r2 = jnp.sum(jnp.where(mask2, excl, 0.0), axis=1, keepdims=True).astype(jnp.int32)
    run_ref[...] += jnp.sum(onehot, axis=0, keepdims=True)
    cnt_ref[0] = run_ref[...].astype(jnp.int32)
    gb1 = pltpu.bitcast(jnp.broadcast_to(g1, (tt, LANES)), jnp.int32)
    gb2 = pltpu.bitcast(jnp.broadcast_to(g2, (tt, LANES)), jnp.int32)
    rec = jnp.where(lane == 0, i1, jnp.where(lane == 1, i2, jnp.where(lane == 2, r1, jnp.where(
        lane == 3, r2, jnp.where(lane == 4, gb1, jnp.where(lane == 5, gb2, 0))))))
    meta_ref[0] = rec.T[:META_ROWS, :]


def _router(x, g, w_router):
    T, D = x.shape
    nt = T // TOK_TILE
    wr = jnp.zeros((D, LANES), F32).at[:, :N_EXPERTS].set(w_router)
    wr_hi = wr.astype(BF16)
    wr = jnp.stack([wr_hi, (wr - wr_hi.astype(F32)).astype(BF16)])
    return pl.pallas_call(
        _router_kernel,
        grid=(nt,),
        in_specs=[pl.BlockSpec((TOK_TILE, D), lambda s: (s, 0)),
                  pl.BlockSpec((1, D), lambda s: (0, 0)),
                  pl.BlockSpec((2, D, LANES), lambda s: (0, 0, 0))],
        out_specs=[pl.BlockSpec((1, META_ROWS, TOK_TILE), lambda s: (s, 0, 0)),
                   pl.BlockSpec((1, 1, LANES), lambda s: (s, 0, 0))],
        out_shape=[jax.ShapeDtypeStruct((nt, META_ROWS, TOK_TILE), jnp.int32),
                   jax.ShapeDtypeStruct((nt, 1, LANES), jnp.int32)],
        scratch_shapes=[pltpu.VMEM((1, LANES), F32)],
        compiler_params=_cparams("arbitrary"),
        name="router",
    )(x, g.reshape(1, D), wr)


def _dispatch_kernel(pad_ref, dest_ref, x_ref, g_ref, wg_ref, wu_ref, wd_ref,
                     xin_hbm, wgo_ref, wuo_ref, wdo_ref, hbuf, sem, zsem, *, nt):
    s = pl.program_id(0)
    slot = s % 2
    hbuf[slot] = _rms(x_ref[...], g_ref[...])

    for t in range(TOK_TILE):
        for k in range(2):
            pltpu.make_async_copy(hbuf.at[slot, pl.ds(t, 1)],
                                  xin_hbm.at[pl.ds(dest_ref[0, 0, k * TOK_TILE + t], 1)],
                                  sem.at[slot]).start(priority=k)

    for w_ref, wo_ref in ((wg_ref, wgo_ref), (wu_ref, wuo_ref), (wd_ref, wdo_ref)):
        wo_ref[...] = w_ref[0].astype(BF16)

    def drain(sl):
        for _ in range(2):
            pltpu.make_async_copy(hbuf.at[sl], xin_hbm.at[pl.ds(0, TOK_TILE)], sem.at[sl]).wait()

    @pl.when(s > 0)
    def _():
        drain(1 - slot)

    @pl.when(s == nt - 1)
    def _():
        drain(slot)
        hbuf[0] = jnp.zeros(hbuf.shape[1:], F32)

        def zissue(r, c):
            pltpu.make_async_copy(hbuf.at[0, pl.ds(0, 1)], xin_hbm.at[pl.ds(r, 1)], zsem).start()
            return c

        def zwait(r, c):
            pltpu.make_async_copy(hbuf.at[0, pl.ds(0, 1)], xin_hbm.at[pl.ds(0, 1)], zsem).wait()
            return c

        for e in range(N_EXPERTS):
            lo, hi = pad_ref[e], pad_ref[N_EXPERTS + e]
            lax.fori_loop(lo, hi, zissue, 0)
            lax.fori_loop(lo, hi, zwait, 0)

        def tissue(r, c):
            pltpu.make_async_copy(hbuf.at[0], xin_hbm.at[pl.ds(pl.multiple_of(r * TOK_TILE, TOK_TILE), TOK_TILE)],
                                  zsem).start()
            return c

        def twait(r, c):
            pltpu.make_async_copy(hbuf.at[0], xin_hbm.at[pl.ds(0, TOK_TILE)], zsem).wait()
            return c

        tail_lo, tail_hi = pad_ref[2 * N_EXPERTS - 1] // TOK_TILE, xin_hbm.shape[0] // TOK_TILE
        lax.fori_loop(tail_lo, tail_hi, tissue, 0)
        lax.fori_loop(tail_lo, tail_hi, twait, 0)


def _dispatch(x, g, dest, pad_info, n_rows, layer, w_gate, w_up, w_down):
    T, D = x.shape
    nt = T // TOK_TILE
    _, E, K, F = w_gate.shape
    flat = lambda w: w.reshape(w.shape[0], E * w.shape[2], w.shape[3])
    ck, cf = E * K // nt, E * F // nt
    w_in_spec = lambda rows, cols: pl.BlockSpec((1, rows, cols), lambda s, pad: (layer, s, 0))
    w_out_spec = lambda rows, cols: pl.BlockSpec((rows, cols), lambda s, pad: (s, 0))
    xin, wg, wu, wd = pl.pallas_call(
        functools.partial(_dispatch_kernel, nt=nt),
        grid_spec=pltpu.PrefetchScalarGridSpec(
            num_scalar_prefetch=1,
            grid=(nt,),
            in_specs=[pl.BlockSpec((1, 1, 2 * TOK_TILE), lambda s, pad: (s, 0, 0), memory_space=pltpu.SMEM),
                      pl.BlockSpec((TOK_TILE, D), lambda s, pad: (s, 0)),
                      pl.BlockSpec((1, D), lambda s, pad: (0, 0)),
                      w_in_spec(ck, F), w_in_spec(ck, F), w_in_spec(cf, K)],
            out_specs=[pl.BlockSpec(memory_space=pl.ANY),
                       w_out_spec(ck, F), w_out_spec(ck, F), w_out_spec(cf, K)],
            scratch_shapes=[pltpu.VMEM((2, TOK_TILE, D), F32),
                            pltpu.SemaphoreType.DMA((2,)), pltpu.SemaphoreType.DMA(())]),
        out_shape=[jax.ShapeDtypeStruct((n_rows, D), F32),
                   jax.ShapeDtypeStruct((E * K, F), BF16), jax.ShapeDtypeStruct((E * K, F), BF16),
                   jax.ShapeDtypeStruct((E * F, K), BF16)],
        compiler_params=_cparams("arbitrary"),
        name="moe_dispatch",
    )(pad_info, dest, x, g.reshape(1, D), flat(w_gate), flat(w_up), flat(w_down))
    return xin, wg.reshape(E, K, F), wu.reshape(E, K, F), wd.reshape(E, F, K)


def _moe_ffn_kernel(be_ref, bv_ref, x_ref, wg_ref, wu_ref, wd_ref, o_ref, xb_ref):
    b, f = pl.program_id(0), pl.program_id(1)

    @pl.when(f == 0)
    def _():
        xb_ref[...] = x_ref[...].astype(BF16)
        o_ref[...] = jnp.zeros_like(o_ref)

    @pl.when(bv_ref[b] != 0)
    def _():
        x = xb_ref[...]
        a = _silu(jnp.dot(x, wg_ref[0], preferred_element_type=F32)) * jnp.dot(x, wu_ref[0], preferred_element_type=F32)
        o_ref[...] += jnp.dot(a.astype(BF16), wd_ref[0], preferred_element_type=F32)


def _moe_ffn(xin, wg, wu, wd, block_e, block_valid, tf=1792):
    R, D = xin.shape
    F = wg.shape[2]
    nb, nf = R // ROW_BLK, F // tf
    fsel = lambda b, f, bv: jnp.where(bv[b] != 0, f, nf - 1)
    return pl.pallas_call(
        _moe_ffn_kernel,
        grid_spec=pltpu.PrefetchScalarGridSpec(
            num_scalar_prefetch=2,
            grid=(nb, nf),
            in_specs=[pl.BlockSpec((ROW_BLK, D), lambda b, f, be, bv: (jnp.where(bv[b] != 0, b, 0), 0)),
                      pl.BlockSpec((1, D, tf), lambda b, f, be, bv: (be[b], 0, fsel(b, f, bv))),
                      pl.BlockSpec((1, D, tf), lambda b, f, be, bv: (be[b], 0, fsel(b, f, bv))),
                      pl.BlockSpec((1, tf, D), lambda b, f, be, bv: (be[b], fsel(b, f, bv), 0))],
            out_specs=pl.BlockSpec((ROW_BLK, D), lambda b, f, be, bv: (b, 0)),
            scratch_shapes=[pltpu.VMEM((ROW_BLK, D), BF16)]),
        out_shape=jax.ShapeDtypeStruct((R, D), F32),
        compiler_params=_cparams("arbitrary", "arbitrary"),
        name="moe_ffn",
    )(block_e, block_valid, xin, wg, wu, wd)


def _combine_kernel(dcur_ref, dnext_ref, x_ref, g1_ref, g2_ref, y_hbm, p_ref, gp_ref, wpg_ref, wpp_ref, gf_ref,
                    o_ref, ya, yb, sem, *, nt, final):
    s = pl.program_id(0)
    slot = s % 2

    def issue(dref, sl, t):
        pltpu.make_async_copy(y_hbm.at[pl.ds(dref[0, 0, t], 1)], ya.at[sl, pl.ds(t, 1)], sem.at[sl]).start()
        pltpu.make_async_copy(y_hbm.at[pl.ds(dref[0, 0, TOK_TILE + t], 1)], yb.at[sl, pl.ds(t, 1)],
                              sem.at[sl]).start(priority=1)

    def drain(sl):
        for buf in (ya, yb):
            pltpu.make_async_copy(y_hbm.at[pl.ds(0, TOK_TILE)], buf.at[sl], sem.at[sl]).wait()

    @pl.when(s == 0)
    def _():
        lax.fori_loop(0, TOK_TILE, lambda t, c: issue(dcur_ref, slot, t) or c, 0, unroll=8)

    drain(slot)
    for t in range(TOK_TILE):
        issue(dnext_ref, 1 - slot, t)
    x = x_ref[...] + g1_ref[...] * ya[slot] + g2_ref[...] * yb[slot]
    o_ref[...] = _ple_tail(x, p_ref, gp_ref, wpg_ref, wpp_ref, gf_ref, final)

    @pl.when(s == nt - 1)
    def _():
        drain(1 - slot)


def _combine(x, y, dest, g1, g2, ple):
    T, D = x.shape
    nt = T // TOK_TILE
    smem_spec = lambda imap: pl.BlockSpec((1, 1, 2 * TOK_TILE), imap, memory_space=pltpu.SMEM)
    col_spec = pl.BlockSpec((TOK_TILE, 1), lambda s: (s, 0))
    return pl.pallas_call(
        functools.partial(_combine_kernel, nt=nt, final=ple.final),
        grid=(nt,),
        in_specs=[smem_spec(lambda s: (s, 0, 0)),
                  smem_spec(lambda s: (jnp.minimum(s + 1, nt - 1), 0, 0)),
                  pl.BlockSpec((TOK_TILE, D), lambda s: (s, 0)),
                  col_spec, col_spec,
                  pl.BlockSpec(memory_space=pl.ANY)] + ple.specs(TOK_TILE, lambda s: s),
        out_specs=pl.BlockSpec((TOK_TILE, D), lambda s: (s, 0)),
        out_shape=jax.ShapeDtypeStruct((T, D), F32),
        scratch_shapes=[pltpu.VMEM((2, TOK_TILE, D), F32), pltpu.VMEM((2, TOK_TILE, D), F32),
                        pltpu.SemaphoreType.DMA((2,))],
        compiler_params=_cparams("arbitrary"),
        name="moe_combine_ple",
    )(dest, dest, x, g1.reshape(T, 1), g2.reshape(T, 1), y, *ple.operands)


def _moe_layer(x, g, w_router, layer, w_gate, w_up, w_down, ple):
    T, D = x.shape
    nt = T // TOK_TILE
    n_blocks = (2 * T + N_EXPERTS * (ROW_BLK - 1) + ROW_BLK - 1) // ROW_BLK
    meta, cnt = _router(x, g, w_router)
    e1, e2, r1, r2 = meta[:, 0], meta[:, 1], meta[:, 2], meta[:, 3]
    g1 = lax.bitcast_convert_type(meta[:, 4], F32)
    g2 = lax.bitcast_convert_type(meta[:, 5], F32)
    counts = cnt[-1, 0, :N_EXPERTS]
    padded = ((counts + ROW_BLK - 1) // ROW_BLK) * ROW_BLK
    pad_end = jnp.cumsum(padded)
    pad_start = pad_end - padded
    dest = jnp.concatenate([pad_start[e1] + r1, pad_start[e2] + r2], axis=1).reshape(nt, 1, 2 * TOK_TILE)
    pad_info = jnp.concatenate([pad_start + counts, pad_end]).astype(jnp.int32)
    xin, wg, wu, wd = _dispatch(x, g, dest, pad_info, n_blocks * ROW_BLK, layer, w_gate, w_up, w_down)
    bstart = jnp.arange(n_blocks, dtype=jnp.int32) * ROW_BLK
    block_e = jnp.minimum(jnp.sum(bstart[:, None] >= pad_end[None, :], axis=1), N_EXPERTS - 1).astype(jnp.int32)
    block_valid = (bstart < pad_end[-1]).astype(jnp.int32)
    y = _moe_ffn(xin, wg, wu, wd, block_e, block_valid)
    return _combine(x, y, dest, g1.reshape(T), g2.reshape(T), ple)


def kernel(x, p, g_mix, w_in, conv_w, conv_b, conv_ln_g, conv_ln_b, w_conv_out, w_ret_out, w_out, g_ffn, w_dense_gate, w_dense_up, w_dense_down, w_router, w_exp_gate, w_exp_up, w_exp_down, g_ple, w_ple_gate, w_ple_proj, g_final):
    B, S, D = x.shape
    T = B * S
    depth = w_in.shape[0]
    tables, g_chunk = _retention_tables(S)
    bf = lambda w: w.astype(BF16)
    xf = x.reshape(T, D)
    pf = p.reshape(depth, T, p.shape[-1])
    for i in range(depth):
        u = _inproj(xf, g_mix[i], bf(w_in[i]))
        u3 = u.reshape(B, S, u.shape[1])
        xf = _mixer(xf.reshape(B, S, D), u3, tables, g_chunk, conv_w[i], conv_b[i], conv_ln_g[i], conv_ln_b[i],
                    bf(w_conv_out[i]), bf(w_ret_out[i]), bf(w_out[i])).reshape(T, D)
        j = i // 2
        ple = _Ple(pf, i, g_ple[i], bf(w_ple_gate[i]), bf(w_ple_proj[i]), g_final, i == depth - 1)
        if i % 2 == 0:
            xf = _dense_ffn(xf, g_ffn[i], bf(w_dense_gate[j]), bf(w_dense_up[j]), bf(w_dense_down[j]), ple)
        else:
            xf = _moe_layer(xf, g_ffn[i], w_router[j], j, w_exp_gate, w_exp_up, w_exp_down, ple)
    return xf.reshape(B, S, D)
```

```python
import functools

import numpy as np
import jax
import jax.numpy as jnp
from jax import lax
from jax.experimental import pallas as pl
from jax.experimental.pallas import tpu as pltpu

F32 = jnp.float32
BF16 = jnp.bfloat16

EPS = 1e-6
CONV_WIDTH = 31
RET_HEADS = 8
RET_DK = 64
RET_DV = 128
RET_CHUNK = 128
ROPE_BASE = 10000.0
N_EXPERTS = 8

V7X_VMEM_LIMIT_BYTES = 56 * 1024 * 1024
LANES = 128
SUBLANES = 8
CONV_HALO = 32
TOK_TILE = 512
ROW_BLK = 512


def _cparams(*sem):
    return pltpu.CompilerParams(dimension_semantics=sem, vmem_limit_bytes=V7X_VMEM_LIMIT_BYTES)


def _rms(x, g):
    return x * lax.rsqrt(jnp.mean(x * x, axis=-1, keepdims=True) + EPS) * g


def _silu(x):
    return x * jax.nn.sigmoid(x)


def _inproj_kernel(x_ref, g_ref, w_ref, o_ref, h_ref):
    @pl.when(pl.program_id(1) == 0)
    def _():
        h_ref[...] = _rms(x_ref[...], g_ref[...]).astype(BF16)

    o_ref[...] = jnp.dot(h_ref[...], w_ref[...], preferred_element_type=F32).astype(BF16)


def _inproj(x, g, w, tm=1024, tn=3072):
    T, D = x.shape
    N = w.shape[1]
    tm = min(tm, T)
    return pl.pallas_call(
        _inproj_kernel,
        grid=(T // tm, N // tn),
        in_specs=[pl.BlockSpec((tm, D), lambda i, j: (i, 0)),
                  pl.BlockSpec((1, D), lambda i, j: (0, 0)),
                  pl.BlockSpec((D, tn), lambda i, j: (0, j))],
        out_specs=pl.BlockSpec((tm, tn), lambda i, j: (i, j)),
        out_shape=jax.ShapeDtypeStruct((T, N), BF16),
        scratch_shapes=[pltpu.VMEM((tm, D), BF16)],
        compiler_params=_cparams("parallel", "arbitrary"),
        name="inproj",
    )(x, g.reshape(1, D), w)


CONV_ROW_CHUNK = 128


def _glu(t):
    half = t.shape[1] // 2
    t = t.astype(F32)
    return t[:, :half] * jax.nn.sigmoid(t[:, half:])


def _conv_swish(cur, halo_z, w_ref, b_ref, lg_ref, lb_ref, zbuf, cbuf):
    ts, cc, rc = cbuf.shape[0], cbuf.shape[1], CONV_ROW_CHUNK
    zbuf[CONV_HALO:CONV_HALO + ts, :] = _glu(cur)
    zbuf[:CONV_HALO, :] = halo_z
    zbuf[CONV_HALO + ts:, :] = jnp.zeros((2 * SUBLANES, cc), F32)
    first = CONV_HALO - (CONV_WIDTH - 1)
    for c in range(cc // LANES):
        cs = slice(c * LANES, (c + 1) * LANES)
        for r in range(ts // rc):
            acc = jnp.broadcast_to(b_ref[:, cs], (rc, LANES))
            for b in range(SUBLANES):
                part = None
                for a in range((first + CONV_WIDTH - 1) // SUBLANES + 1):
                    j = SUBLANES * a + b - first
                    if 0 <= j < CONV_WIDTH:
                        lo = r * rc + SUBLANES * a
                        term = w_ref[j:j + 1, cs] * zbuf[lo:lo + rc + SUBLANES, cs]
                        part = term if part is None else part + term
                acc = acc + part[b:b + rc]
            cbuf[r * rc:(r + 1) * rc, cs] = acc
    z = cbuf[...]
    mu = jnp.mean(z, axis=-1, keepdims=True)
    var = jnp.mean(jnp.square(z - mu), axis=-1, keepdims=True)
    y = (z - mu) * lax.rsqrt(var + EPS) * lg_ref[...] + lb_ref[...]
    return _silu(y).astype(BF16)


def _retention_rows(q_in, k_in, v_of, gr_of, cos, sa, sb, dmat_ref, xi_ref, zeta_ref, store, state, g_chunk):
    C, pw, vw = RET_CHUNK, 2 * RET_DK, 2 * RET_DV
    hdk = RET_HEADS * RET_DK
    npair = RET_HEADS // 2
    nch = q_in.shape[0] // C
    cos, sa, sb = (jnp.concatenate([t] * npair, axis=1) for t in (cos, sa, sb))

    def rope(t):
        return t * cos + pltpu.roll(t, hdk - RET_DK // 2, 1) * sa + pltpu.roll(t, RET_DK // 2, 1) * sb

    qf = rope(q_in.astype(F32)) * (RET_DK ** -0.5)
    kf = rope(k_in.astype(F32))
    q = qf.astype(BF16)
    k = kf.astype(BF16)
    lane = lax.broadcasted_iota(jnp.int32, (C, pw), 1)
    keep0 = jnp.where(lane < RET_DK, 1.0, 0.0).astype(BF16)
    keep1 = jnp.where(lane >= RET_DK, 1.0, 0.0).astype(BF16)
    row = lax.broadcasted_iota(jnp.int32, (pw, vw), 0)
    col = lax.broadcasted_iota(jnp.int32, (pw, vw), 1)
    top = row < RET_DK
    diag = top == (col < RET_DV)
    zeros_v = jnp.zeros((C, RET_DV), BF16)
    nt_dims = (((1,), (1,)), ((), ()))
    tn_dims = (((0,), (0,)), ((), ()))

    scores = []
    for ci in range(nch):
        rs = slice(ci * C, (ci + 1) * C)
        for p in range(npair):
            kp = k[rs, p * pw:(p + 1) * pw]
            kcat = jnp.concatenate([kp * keep0, kp * keep1], axis=0)
            s = lax.dot_general(q[rs, p * pw:(p + 1) * pw], kcat, nt_dims, preferred_element_type=F32)
            scores.append((s * dmat_ref[p]).astype(BF16))

    st = [state[p] for p in range(npair)]
    before = []
    for ci in range(nch):
        rs = slice(ci * C, (ci + 1) * C)
        kz = (kf[rs] * zeta_ref[...]).astype(BF16)
        for p in range(npair):
            before.append(st[p].astype(BF16))
            upd = lax.dot_general(kz[:, p * pw:(p + 1) * pw], v_of(rs, slice(p * vw, (p + 1) * vw)), tn_dims,
                                  preferred_element_type=F32)
            st[p] = jnp.where(top, g_chunk[2 * p], g_chunk[2 * p + 1]) * st[p] + jnp.where(diag, upd, 0.0)
    for p in range(npair):
        state[p] = st[p]

    for ci in range(nch):
        rs = slice(ci * C, (ci + 1) * C)
        qx = (qf[rs] * xi_ref[...]).astype(BF16)
        for p in range(npair):
            vp = v_of(rs, slice(p * vw, (p + 1) * vw))
            vbd = jnp.concatenate([jnp.concatenate([vp[:, :RET_DV], zeros_v], axis=1),
                                   jnp.concatenate([zeros_v, vp[:, RET_DV:]], axis=1)], axis=0)
            lhs = jnp.concatenate([scores[ci * npair + p], qx[:, p * pw:(p + 1) * pw]], axis=1)
            rhs = jnp.concatenate([vbd, before[ci * npair + p]], axis=0)
            o2 = jnp.dot(lhs, rhs, preferred_element_type=F32)
            for e in range(2):
                vs = slice((2 * p + e) * RET_DV, (2 * p + e + 1) * RET_DV)
                o = o2[:, e * RET_DV:(e + 1) * RET_DV]
                mu = jnp.mean(o, axis=-1, keepdims=True)
                var = jnp.mean(jnp.square(o - mu), axis=-1, keepdims=True)
                o = (o - mu) * lax.rsqrt(var + EPS)
                store(rs, vs, (_silu(gr_of(rs, vs).astype(F32)) * o).astype(BF16))


def _retention_tables(S):
    C, H = RET_CHUNK, RET_HEADS
    inv_freq = ROPE_BASE ** (-jnp.arange(0, RET_DK, 2, dtype=F32) / RET_DK)
    ang = jnp.arange(S, dtype=F32)[:, None] * inv_freq[None, :]
    cos, sin = jnp.cos(ang), jnp.sin(ang)
    zero = jnp.zeros_like(sin)
    cos_f = jnp.tile(cos, (1, 4))
    sin_a = jnp.tile(jnp.concatenate([-sin, zero], axis=1), (1, 2))
    sin_b = jnp.tile(jnp.concatenate([zero, sin], axis=1), (1, 2))
    log_g = jnp.log(1.0 - jnp.exp2(-5.0 - jnp.arange(H, dtype=F32)))
    pos = jnp.arange(C, dtype=F32)
    diff = pos[:, None] - pos[None, :]
    dmat = jnp.where(diff[None] >= 0, jnp.exp(jnp.maximum(diff, 0.0)[None] * log_g[:, None, None]), 0.0)
    xi = jnp.exp((pos + 1.0)[None] * log_g[:, None])
    zeta = jnp.exp((C - 1.0 - pos)[None] * log_g[:, None])
    xi_f = jnp.repeat(xi.T, RET_DK, axis=1)
    zeta_f = jnp.repeat(zeta.T, RET_DK, axis=1)
    g_chunk = tuple(float(np.exp(C * np.log(1.0 - 2.0 ** (-5.0 - h)))) for h in range(H))
    dmat2 = dmat.reshape(H // 2, 2, C, C).transpose(0, 2, 1, 3).reshape(H // 2, C, 2 * C)
    return (cos_f, sin_a, sin_b, dmat2, xi_f, zeta_f), g_chunk


def _resident(shape):
    return pl.BlockSpec(shape, lambda *_: (0,) * len(shape), pipeline_mode=pl.Buffered(1))


def _mixer_kernel(x_ref, u_ref, halo_ref, cos_ref, sa_ref, sb_ref, dmat_ref, xi_ref, zeta_ref,
                  cw_ref, cb_ref, lg_ref, lb_ref, wc_ref, wr_ref, wo_ref, o_ref, state, og, zbuf, cbuf, *, g_chunk):
    @pl.when(pl.program_id(1) == 0)
    def _():
        state[...] = jnp.zeros_like(state)

    nparts, rows, cc = cbuf.shape[0], cbuf.shape[1], cbuf.shape[2]
    d = x_ref.shape[2]
    hdk, hdv = RET_HEADS * RET_DK, RET_HEADS * RET_DV
    q0 = 2 * cc
    k0, v0 = q0 + hdk, q0 + 2 * hdk
    gr0 = v0 + hdv
    ga0 = gr0 + hdv
    gb0 = ga0 + d
    halo_z = jnp.where(pl.program_id(1) == 0, 0.0, _glu(halo_ref[0]))
    for part in range(nparts):
        r0 = part * rows
        rs = slice(r0, r0 + rows)
        shift = lambda sl, off: slice(sl.start + off, sl.stop + off)
        _retention_rows(
            u_ref[0, rs, q0:k0], u_ref[0, rs, k0:v0],
            lambda r, c: u_ref[0, shift(r, r0), shift(c, v0)],
            lambda r, c: u_ref[0, shift(r, r0), shift(c, gr0)],
            cos_ref[rs, :], sa_ref[rs, :], sb_ref[rs, :], dmat_ref, xi_ref, zeta_ref,
            lambda r, c, val: og.__setitem__((shift(r, r0), c), val), state, g_chunk)
        if part:
            halo_z = _glu(u_ref[0, r0 - CONV_HALO:r0, :q0])
        c = _conv_swish(u_ref[0, rs, :q0], halo_z, cw_ref, cb_ref, lg_ref, lb_ref, zbuf.at[part], cbuf.at[part])
        yc = jnp.dot(c, wc_ref[...], preferred_element_type=F32)
        yr = jnp.dot(og[rs, :], wr_ref[...], preferred_element_type=F32)
        merged = (jax.nn.sigmoid(u_ref[0, rs, ga0:gb0].astype(F32)) * yc
                  + jax.nn.sigmoid(u_ref[0, rs, gb0:gb0 + d].astype(F32)) * yr)
        o_ref[0, rs, :] = x_ref[0, rs, :] + jnp.dot(merged.astype(BF16), wo_ref[...], preferred_element_type=F32)


def _mixer(x3, u3, tables, g_chunk, conv_w, conv_b, ln_g, ln_b, wc, wr, wo, tm=512, nparts=2):
    B, S, D = x3.shape
    cc = conv_w.shape[1]
    ucols = u3.shape[2]
    tm = min(tm, S)
    hb = tm // CONV_HALO
    hdk, hdv = RET_HEADS * RET_DK, RET_HEADS * RET_DV
    C = RET_CHUNK
    seq = lambda cols: pl.BlockSpec((tm, cols), lambda b, i: (i, 0))
    return pl.pallas_call(
        functools.partial(_mixer_kernel, g_chunk=g_chunk),
        grid=(B, S // tm),
        in_specs=[pl.BlockSpec((1, tm, D), lambda b, i: (b, i, 0)),
                  pl.BlockSpec((1, tm, ucols), lambda b, i: (b, i, 0)),
                  pl.BlockSpec((1, CONV_HALO, 2 * cc), lambda b, i: (b, jnp.maximum(i * hb - 1, 0), 0)),
                  seq(2 * RET_DK), seq(2 * RET_DK), seq(2 * RET_DK),
                  _resident((RET_HEADS // 2, C, 2 * C)), _resident((C, hdk)), _resident((C, hdk)),
                  _resident((CONV_WIDTH, cc)), _resident((1, cc)), _resident((1, cc)), _resident((1, cc)),
                  _resident((cc, D)), _resident((hdv, D)), _resident((D, D))],
        out_specs=pl.BlockSpec((1, tm, D), lambda b, i: (b, i, 0)),
        out_shape=jax.ShapeDtypeStruct((B, S, D), F32),
        scratch_shapes=[pltpu.VMEM((RET_HEADS // 2, 2 * RET_DK, 2 * RET_DV), F32),
                        pltpu.VMEM((tm, hdv), BF16),
                        pltpu.VMEM((nparts, tm // nparts + CONV_HALO + 2 * SUBLANES, cc), F32),
                        pltpu.VMEM((nparts, tm // nparts, cc), F32)],
        compiler_params=_cparams("parallel", "arbitrary"),
        name="mixer",
    )(x3, u3, u3, *tables, conv_w, conv_b.reshape(1, cc), ln_g.reshape(1, cc), ln_b.reshape(1, cc), wc, wr, wo)


class _Ple:
    def __init__(self, p, layer, g, wgate, wproj, g_final, final):
        D = wgate.shape[0]
        self.final = final
        self.operands = (p, g.reshape(1, D), wgate, wproj, g_final.reshape(1, D))
        self.layer, self.pdim, self.d = layer, p.shape[2], D

    def specs(self, rows, row_map):
        return [pl.BlockSpec((1, rows, self.pdim), lambda *a: (self.layer, row_map(*a), 0)),
                _resident((1, self.d)), _resident((self.d, self.d)), _resident((self.pdim, self.d)),
                _resident((1, self.d))]


def _ple_tail(x, p_ref, g_ref, wgate_ref, wproj_ref, gf_ref, final):
    h = _rms(x, g_ref[...]).astype(BF16)
    gate = jax.nn.sigmoid(jnp.dot(h, wgate_ref[...], preferred_element_type=F32))
    proj = jnp.dot(p_ref[0].astype(BF16), wproj_ref[...], preferred_element_type=F32)
    y = x + gate * proj
    return _rms(y, gf_ref[...]) if final else y


def _dense_ffn_kernel(x_ref, g_ref, wg_ref, wu_ref, wd_ref, p_ref, gp_ref, wpg_ref, wpp_ref, gf_ref, o_ref, *, final):
    x = x_ref[...]
    h = _rms(x, g_ref[...]).astype(BF16)
    a = _silu(jnp.dot(h, wg_ref[...], preferred_element_type=F32)) * jnp.dot(h, wu_ref[...], preferred_element_type=F32)
    x = x + jnp.dot(a.astype(BF16), wd_ref[...], preferred_element_type=F32)
    o_ref[...] = _ple_tail(x, p_ref, gp_ref, wpg_ref, wpp_ref, gf_ref, final)


def _dense_ffn(x, g, wg, wu, wd, ple, tm=512):
    T, D = x.shape
    F = wg.shape[1]
    tm = min(tm, T)
    return pl.pallas_call(
        functools.partial(_dense_ffn_kernel, final=ple.final),
        grid=(T // tm,),
        in_specs=[pl.BlockSpec((tm, D), lambda i: (i, 0)),
                  _resident((1, D)), _resident((D, F)), _resident((D, F)), _resident((F, D))]
                 + ple.specs(tm, lambda i: i),
        out_specs=pl.BlockSpec((tm, D), lambda i: (i, 0)),
        out_shape=jax.ShapeDtypeStruct((T, D), F32),
        compiler_params=_cparams("parallel"),
        name="dense_ffn_ple",
    )(x, g.reshape(1, D), wg, wu, wd, *ple.operands)


META_ROWS = 8


def _router_kernel(x_ref, g_ref, wr_ref, meta_ref, cnt_ref, run_ref):
    @pl.when(pl.program_id(0) == 0)
    def _():
        run_ref[...] = jnp.zeros_like(run_ref)

    h = _rms(x_ref[...], g_ref[...])
    h_hi = h.astype(BF16)
    h_lo = (h - h_hi.astype(F32)).astype(BF16)
    w_hi, w_lo = wr_ref[0], wr_ref[1]
    logits = (jnp.dot(h_hi, w_hi, preferred_element_type=F32) + jnp.dot(h_lo, w_hi, preferred_element_type=F32)
              + jnp.dot(h_hi, w_lo, preferred_element_type=F32))
    tt = logits.shape[0]
    lane = lax.broadcasted_iota(jnp.int32, (tt, LANES), 1)
    neg = jnp.float32(-jnp.inf)
    logits = jnp.where(lane < N_EXPERTS, logits, neg)
    m1 = jnp.max(logits, axis=1, keepdims=True)
    i1 = jnp.min(jnp.where(logits == m1, lane, LANES), axis=1, keepdims=True)
    mask1 = lane == i1
    rest = jnp.where(mask1, neg, logits)
    m2 = jnp.max(rest, axis=1, keepdims=True)
    i2 = jnp.min(jnp.where(rest == m2, lane, LANES), axis=1, keepdims=True)
    mask2 = lane == i2
    d = jnp.exp(m2 - m1)
    g1 = 1.0 / (1.0 + d)
    g2 = d / (1.0 + d)
    onehot = jnp.where(mask1 | mask2, 1.0, 0.0)
    row = lax.broadcasted_iota(jnp.int32, (tt, tt), 0)
    col = lax.broadcasted_iota(jnp.int32, (tt, tt), 1)
    lower = jnp.where(col < row, 1.0, 0.0).astype(BF16)
    excl = jnp.dot(lower, onehot.astype(BF16), preferred_element_type=F32) + run_ref[...]
    r1 = jnp.sum(jnp.where(mask1, excl, 0.0), axis=1, keepdims=True).astype(jnp.int32)
    r2 = jnp.sum(jnp.where(mask2, excl, 0.0), axis=1, keepdims=True).astype(jnp.int32)
    run_ref[...] += jnp.sum(onehot, axis=0, keepdims=True)
    cnt_ref[0] = run_ref[...].astype(jnp.int32)
    gb1 = pltpu.bitcast(jnp.broadcast_to(g1, (tt, LANES)), jnp.int32)
    gb2 = pltpu.bitcast(jnp.broadcast_to(g2, (tt, LANES)), jnp.int32)
    rec = jnp.where(lane == 0, i1, jnp.where(lane == 1, i2, jnp.where(lane == 2, r1, jnp.where(
        lane == 3, r2, jnp.where(lane == 4, gb1, jnp.where(lane == 5, gb2, 0))))))
    meta_ref[0] = rec.T[:META_ROWS, :]


def _router(x, g, w_router):
    T, D = x.shape
    nt = T // TOK_TILE
    wr = jnp.zeros((D, LANES), F32).at[:, :N_EXPERTS].set(w_router)
    wr_hi = wr.astype(BF16)
    wr = jnp.stack([wr_hi, (wr - wr_hi.astype(F32)).astype(BF16)])
    return pl.pallas_call(
        _router_kernel,
        grid=(nt,),
        in_specs=[pl.BlockSpec((TOK_TILE, D), lambda s: (s, 0)),
                  pl.BlockSpec((1, D), lambda s: (0, 0)),
                  pl.BlockSpec((2, D, LANES), lambda s: (0, 0, 0))],
        out_specs=[pl.BlockSpec((1, META_ROWS, TOK_TILE), lambda s: (s, 0, 0)),
                   pl.BlockSpec((1, 1, LANES), lambda s: (s, 0, 0))],
        out_shape=[jax.ShapeDtypeStruct((nt, META_ROWS, TOK_TILE), jnp.int32),
                   jax.ShapeDtypeStruct((nt, 1, LANES), jnp.int32)],
        scratch_shapes=[pltpu.VMEM((1, LANES), F32)],
        compiler_params=_cparams("arbitrary"),
        name="router",
    )(x, g.reshape(1, D), wr)


def _dispatch_kernel(pad_ref, dest_ref, x_ref, g_ref, wg_ref, wu_ref, wd_ref,
                     xin_hbm, wgo_ref, wuo_ref, wdo_ref, hbuf, sem, zsem, *, nt):
    s = pl.program_id(0)
    slot = s % 2
    hbuf[slot] = _rms(x_ref[...], g_ref[...])

    for t in range(TOK_TILE):
        for k in range(2):
            pltpu.make_async_copy(hbuf.at[slot, pl.ds(t, 1)],
                                  xin_hbm.at[pl.ds(dest_ref[0, 0, k * TOK_TILE + t], 1)],
                                  sem.at[slot]).start(priority=k)

    for w_ref, wo_ref in ((wg_ref, wgo_ref), (wu_ref, wuo_ref), (wd_ref, wdo_ref)):
        wo_ref[...] = w_ref[0].astype(BF16)

    def drain(sl):
        for _ in range(2):
            pltpu.make_async_copy(hbuf.at[sl], xin_hbm.at[pl.ds(0, TOK_TILE)], sem.at[sl]).wait()

    @pl.when(s > 0)
    def _():
        drain(1 - slot)

    @pl.when(s == nt - 1)
    def _():
        drain(slot)
        hbuf[0] = jnp.zeros(hbuf.shape[1:], F32)

        def zissue(r, c):
            pltpu.make_async_copy(hbuf.at[0, pl.ds(0, 1)], xin_hbm.at[pl.ds(r, 1)], zsem).start()
            return c

        def zwait(r, c):
            pltpu.make_async_copy(hbuf.at[0, pl.ds(0, 1)], xin_hbm.at[pl.ds(0, 1)], zsem).wait()
            return c

        for e in range(N_EXPERTS):
            lo, hi = pad_ref[e], pad_ref[N_EXPERTS + e]
            lax.fori_loop(lo, hi, zissue, 0)
            lax.fori_loop(lo, hi, zwait, 0)

        def tissue(r, c):
            pltpu.make_async_copy(hbuf.at[0], xin_hbm.at[pl.ds(pl.multiple_of(r * TOK_TILE, TOK_TILE), TOK_TILE)],
                                  zsem).start()
            return c

        def twait(r, c):
            pltpu.make_async_copy(hbuf.at[0], xin_hbm.at[pl.ds(0, TOK_TILE)], zsem).wait()
            return c

        tail_lo, tail_hi = pad_ref[2 * N_EXPERTS - 1] // TOK_TILE, xin_hbm.shape[0] // TOK_TILE
        lax.fori_loop(tail_lo, tail_hi, tissue, 0)
        lax.fori_loop(tail_lo, tail_hi, twait, 0)


def _dispatch(x, g, dest, pad_info, n_rows, layer, w_gate, w_up, w_down):
    T, D = x.shape
    nt = T // TOK_TILE
    _, E, K, F = w_gate.shape
    flat = lambda w: w.reshape(w.shape[0], E * w.shape[2], w.shape[3])
    ck, cf = E * K // nt, E * F // nt
    w_in_spec = lambda rows, cols: pl.BlockSpec((1, rows, cols), lambda s, pad: (layer, s, 0))
    w_out_spec = lambda rows, cols: pl.BlockSpec((rows, cols), lambda s, pad: (s, 0))
    xin, wg, wu, wd = pl.pallas_call(
        functools.partial(_dispatch_kernel, nt=nt),
        grid_spec=pltpu.PrefetchScalarGridSpec(
            num_scalar_prefetch=1,
            grid=(nt,),
            in_specs=[pl.BlockSpec((1, 1, 2 * TOK_TILE), lambda s, pad: (s, 0, 0), memory_space=pltpu.SMEM),
                      pl.BlockSpec((TOK_TILE, D), lambda s, pad: (s, 0)),
                      pl.BlockSpec((1, D), lambda s, pad: (0, 0)),
                      w_in_spec(ck, F), w_in_spec(ck, F), w_in_spec(cf, K)],
            out_specs=[pl.BlockSpec(memory_space=pl.ANY),
                       w_out_spec(ck, F), w_out_spec(ck, F), w_out_spec(cf, K)],
            scratch_shapes=[pltpu.VMEM((2, TOK_TILE, D), F32),
                            pltpu.SemaphoreType.DMA((2,)), pltpu.SemaphoreType.DMA(())]),
        out_shape=[jax.ShapeDtypeStruct((n_rows, D), F32),
                   jax.ShapeDtypeStruct((E * K, F), BF16), jax.ShapeDtypeStruct((E * K, F), BF16),
                   jax.ShapeDtypeStruct((E * F, K), BF16)],
        compiler_params=_cparams("arbitrary"),
        name="moe_dispatch",
    )(pad_info, dest, x, g.reshape(1, D), flat(w_gate), flat(w_up), flat(w_down))
    return xin, wg.reshape(E, K, F), wu.reshape(E, K, F), wd.reshape(E, F, K)


def _moe_ffn_kernel(be_ref, bv_ref, x_ref, wg_ref, wu_ref, wd_ref, o_ref, xb_ref):
    b, f = pl.program_id(0), pl.program_id(1)

    @pl.when(f == 0)
    def _():
        xb_ref[...] = x_ref[...].astype(BF16)
        o_ref[...] = jnp.zeros_like(o_ref)

    @pl.when(bv_ref[b] != 0)
    def _():
        x = xb_ref[...]
        a = _silu(jnp.dot(x, wg_ref[0], preferred_element_type=F32)) * jnp.dot(x, wu_ref[0], preferred_element_type=F32)
        o_ref[...] += jnp.dot(a.astype(BF16), wd_ref[0], preferred_element_type=F32)


def _moe_ffn(xin, wg, wu, wd, block_e, block_valid, tf=1792):
    R, D = xin.shape
    F = wg.shape[2]
    nb, nf = R // ROW_BLK, F // tf
    fsel = lambda b, f, bv: jnp.where(bv[b] != 0, f, nf - 1)
    return pl.pallas_call(
        _moe_ffn_kernel,
        grid_spec=pltpu.PrefetchScalarGridSpec(
            num_scalar_prefetch=2,
            grid=(nb, nf),
            in_specs=[pl.BlockSpec((ROW_BLK, D), lambda b, f, be, bv: (jnp.where(bv[b] != 0, b, 0), 0)),
                      pl.BlockSpec((1, D, tf), lambda b, f, be, bv: (be[b], 0, fsel(b, f, bv))),
                      pl.BlockSpec((1, D, tf), lambda b, f, be, bv: (be[b], 0, fsel(b, f, bv))),
                      pl.BlockSpec((1, tf, D), lambda b, f, be, bv: (be[b], fsel(b, f, bv), 0))],
            out_specs=pl.BlockSpec((ROW_BLK, D), lambda b, f, be, bv: (b, 0)),
            scratch_shapes=[pltpu.VMEM((ROW_BLK, D), BF16)]),
        out_shape=jax.ShapeDtypeStruct((R, D), F32),
        compiler_params=_cparams("arbitrary", "arbitrary"),
        name="moe_ffn",
    )(block_e, block_valid, xin, wg, wu, wd)


def _combine_kernel(dcur_ref, dnext_ref, x_ref, g1_ref, g2_ref, y_hbm, p_ref, gp_ref, wpg_ref, wpp_ref, gf_ref,
                    o_ref, ya, yb, sem, *, nt, final):
    s = pl.program_id(0)
    slot = s % 2

    def issue(dref, sl, t):
        pltpu.make_async_copy(y_hbm.at[pl.ds(dref[0, 0, t], 1)], ya.at[sl, pl.ds(t, 1)], sem.at[sl]).start()
        pltpu.make_async_copy(y_hbm.at[pl.ds(dref[0, 0, TOK_TILE + t], 1)], yb.at[sl, pl.ds(t, 1)],
                              sem.at[sl]).start(priority=1)

    def drain(sl):
        for buf in (ya, yb):
            pltpu.make_async_copy(y_hbm.at[pl.ds(0, TOK_TILE)], buf.at[sl], sem.at[sl]).wait()

    @pl.when(s == 0)
    def _():
        lax.fori_loop(0, TOK_TILE, lambda t, c: issue(dcur_ref, slot, t) or c, 0, unroll=8)

    drain(slot)
    for t in range(TOK_TILE):
        issue(dnext_ref, 1 - slot, t)
    x = x_ref[...] + g1_ref[...] * ya[slot] + g2_ref[...] * yb[slot]
    o_ref[...] = _ple_tail(x, p_ref, gp_ref, wpg_ref, wpp_ref, gf_ref, final)

    @pl.when(s == nt - 1)
    def _():
        drain(1 - slot)


def _combine(x, y, dest, g1, g2, ple):
    T, D = x.shape
    nt = T // TOK_TILE
    smem_spec = lambda imap: pl.BlockSpec((1, 1, 2 * TOK_TILE), imap, memory_space=pltpu.SMEM)
    col_spec = pl.BlockSpec((TOK_TILE, 1), lambda s: (s, 0))
    return pl.pallas_call(
        functools.partial(_combine_kernel, nt=nt, final=ple.final),
        grid=(nt,),
        in_specs=[smem_spec(lambda s: (s, 0, 0)),
                  smem_spec(lambda s: (jnp.minimum(s + 1, nt - 1), 0, 0)),
                  pl.BlockSpec((TOK_TILE, D), lambda s: (s, 0)),
                  col_spec, col_spec,
                  pl.BlockSpec(memory_space=pl.ANY)] + ple.specs(TOK_TILE, lambda s: s),
        out_specs=pl.BlockSpec((TOK_TILE, D), lambda s: (s, 0)),
        out_shape=jax.ShapeDtypeStruct((T, D), F32),
        scratch_shapes=[pltpu.VMEM((2, TOK_TILE, D), F32), pltpu.VMEM((2, TOK_TILE, D), F32),
                        pltpu.SemaphoreType.DMA((2,))],
        compiler_params=_cparams("arbitrary"),
        name="moe_combine_ple",
    )(dest, dest, x, g1.reshape(T, 1), g2.reshape(T, 1), y, *ple.operands)


def _moe_layer(x, g, w_router, layer, w_gate, w_up, w_down, ple):
    T, D = x.shape
    nt = T // TOK_TILE
    n_blocks = (2 * T + N_EXPERTS * (ROW_BLK - 1) + ROW_BLK - 1) // ROW_BLK
    meta, cnt = _router(x, g, w_router)
    e1, e2, r1, r2 = meta[:, 0], meta[:, 1], meta[:, 2], meta[:, 3]
    g1 = lax.bitcast_convert_type(meta[:, 4], F32)
    g2 = lax.bitcast_convert_type(meta[:, 5], F32)
    counts = cnt[-1, 0, :N_EXPERTS]
    padded = ((counts + ROW_BLK - 1) // ROW_BLK) * ROW_BLK
    pad_end = jnp.cumsum(padded)
    pad_start = pad_end - padded
    dest = jnp.concatenate([pad_start[e1] + r1, pad_start[e2] + r2], axis=1).reshape(nt, 1, 2 * TOK_TILE)
    pad_info = jnp.concatenate([pad_start + counts, pad_end]).astype(jnp.int32)
    xin, wg, wu, wd = _dispatch(x, g, dest, pad_info, n_blocks * ROW_BLK, layer, w_gate, w_up, w_down)
    bstart = jnp.arange(n_blocks, dtype=jnp.int32) * ROW_BLK
    block_e = jnp.minimum(jnp.sum(bstart[:, None] >= pad_end[None, :], axis=1), N_EXPERTS - 1).astype(jnp.int32)
    block_valid = (bstart < pad_end[-1]).astype(jnp.int32)
    y = _moe_ffn(xin, wg, wu, wd, block_e, block_valid)
    return _combine(x, y, dest, g1.reshape(T), g2.reshape(T), ple)


def kernel(x, p, g_mix, w_in, conv_w, conv_b, conv_ln_g, conv_ln_b, w_conv_out, w_ret_out, w_out, g_ffn, w_dense_gate, w_dense_up, w_dense_down, w_router, w_exp_gate, w_exp_up, w_exp_down, g_ple, w_ple_gate, w_ple_proj, g_final):
    B, S, D = x.shape
    T = B * S
    depth = w_in.shape[0]
    tables, g_chunk = _retention_tables(S)
    bf = lambda w: w.astype(BF16)
    xf = x.reshape(T, D)
    pf = p.reshape(depth, T, p.shape[-1])
    for i in range(depth):
        u = _inproj(xf, g_mix[i], bf(w_in[i]))
        u3 = u.reshape(B, S, u.shape[1])
        xf = _mixer(xf.reshape(B, S, D), u3, tables, g_chunk, conv_w[i], conv_b[i], conv_ln_g[i], conv_ln_b[i],
                    bf(w_conv_out[i]), bf(w_ret_out[i]), bf(w_out[i])).reshape(T, D)
        j = i // 2
        ple = _Ple(pf, i, g_ple[i], bf(w_ple_gate[i]), bf(w_ple_proj[i]), g_final, i == depth - 1)
        if i % 2 == 0:
            xf = _dense_ffn(xf, g_ffn[i], bf(w_dense_gate[j]), bf(w_dense_up[j]), bf(w_dense_down[j]), ple)
        else:
            xf = _moe_layer(xf, g_ffn[i], w_router[j], j, w_exp_gate, w_exp_up, w_exp_down, ple)
    return xf.reshape(B, S, D)
```
